```python
import math
import jax
import jax.numpy as jnp
from jax import lax
import numpy as np

D_MODEL = 2048
BATCH = 32
SEQ = 256
DEPTH = 4
DEC_BATCH = 8
DEC_SEQ = 2048
PAST_LEN = 512

GRID_W = 64
EPS = 1e-6
F32 = jnp.float32

SSD_HEADS = 16
SSD_HEADDIM = 64
SSD_INNER = SSD_HEADS * SSD_HEADDIM
SSD_GROUPS = 4
D_STATE = 64
CONV_W = 5
CHUNK = 128
XBC_DIM = SSD_INNER + 2 * SSD_GROUPS * D_STATE

N_HEADS = 16
KV_HEADS = 4
HEAD_DIM = 64
ATTN_INNER = N_HEADS * HEAD_DIM
KV_INNER = KV_HEADS * HEAD_DIM
Q_BLOCK = 128
ROPE_THETA = 10000.0

POOL_GROUPS = 4
POOL_INNER = 1024
POOL_GROUP_DIM = POOL_INNER // POOL_GROUPS
POOL_WINDOWS = (2, 4, 8, 16)

N_BRANCHES = 3

N_EXPERTS = 16
N_EXPERT_GROUPS = 4
EXPERTS_PER_GROUP = N_EXPERTS // N_EXPERT_GROUPS
TOP_K = 2
D_FF = 512

IN_SPLITS = (SSD_INNER, XBC_DIM, SSD_HEADS, SSD_HEADS, ATTN_INNER, KV_INNER, KV_INNER, POOL_INNER)
D_IN = SSD_INNER + XBC_DIM + 2 * SSD_HEADS + ATTN_INNER + 2 * KV_INNER + POOL_INNER + N_BRANCHES * D_MODEL

kernel_name = 'hybrid_dit_prefix_step'


def rms_norm(x, g):
    xf = x.astype(F32)
    y = xf * lax.rsqrt(jnp.mean(xf * xf, axis=-1, keepdims=True) + EPS)
    return (y * g).astype(x.dtype)


def modulate(x, gain, shift, scale):
    return rms_norm(x, gain) * (1 + scale[:, None, :]) + shift[:, None, :]


def axial_rope(n_tokens):
    rows = n_tokens // GRID_W
    row = jnp.repeat(jnp.arange(rows, dtype=F32), GRID_W)
    col = jnp.tile(jnp.arange(GRID_W, dtype=F32), rows)
    n_freq = HEAD_DIM // 4
    inv = ROPE_THETA ** (-jnp.arange(n_freq, dtype=F32) / n_freq)
    ang = jnp.concatenate([row[:, None] * inv, col[:, None] * inv], axis=-1)
    return jnp.cos(ang), jnp.sin(ang)


def apply_rope(x, cos, sin):
    xf = x.astype(F32).reshape(x.shape[:-1] + (HEAD_DIM // 2, 2))
    x1, x2 = xf[..., 0], xf[..., 1]
    c, s = cos[None, :, None, :], sin[None, :, None, :]
    out = jnp.stack([x1 * c - x2 * s, x1 * s + x2 * c], axis=-1)
    return out.reshape(x.shape).astype(x.dtype)


def gqa_attend(q, k, v):
    b, tq = q.shape[:2]
    nb = tq // Q_BLOCK
    qb = q.reshape(b, nb, Q_BLOCK, KV_HEADS, N_HEADS // KV_HEADS, HEAD_DIM).transpose(1, 0, 2, 3, 4, 5)
    scale = HEAD_DIM ** -0.5

    def block(qi):
        s = jnp.einsum('bqkgd,bskd->bkgqs', qi, k, preferred_element_type=F32) * scale
        p = jax.nn.softmax(s, axis=-1)
        return jnp.einsum('bkgqs,bskd->bqkgd', p.astype(v.dtype), v)

    o = lax.map(block, qb)
    return o.transpose(1, 0, 2, 3, 4, 5).reshape(b, tq, ATTN_INNER)


def centred_dwconv(x, w, bias):
    pad = CONV_W // 2
    y = lax.conv_general_dilated(x, w[:, None, :], window_strides=(1,), padding=((pad, pad),),
                                 dimension_numbers=('NWC', 'WIO', 'NWC'), feature_group_count=x.shape[-1])
    return y + bias


def segsum(a):
    t = a.shape[-1]
    x = jnp.broadcast_to(a[..., None], a.shape + (t,))
    x = jnp.where(jnp.tril(jnp.ones((t, t), bool), -1), x, 0)
    ss = jnp.cumsum(x, axis=-2)
    return jnp.where(jnp.tril(jnp.ones((t, t), bool)), ss, -jnp.inf)


def ssd_scan(x, dt, a_head, bm, cm, h0):
    b, L = x.shape[:2]
    nc = L // CHUNK
    xd = (x * dt[..., None]).reshape(b, nc, CHUNK, SSD_HEADS, SSD_HEADDIM)
    a = (dt * a_head).reshape(b, nc, CHUNK, SSD_HEADS).transpose(0, 3, 1, 2)
    bc = bm.reshape(b, nc, CHUNK, SSD_HEADS, D_STATE)
    cc = cm.reshape(b, nc, CHUNK, SSD_HEADS, D_STATE)
    a_cs = jnp.cumsum(a, axis=-1)
    lmat = jnp.exp(segsum(a))
    y_diag = jnp.einsum('bclhn,bcshn,bhcls,bcshp->bclhp', cc, bc, lmat, xd)
    decay_states = jnp.exp(a_cs[..., -1:] - a_cs)
    states = jnp.einsum('bclhn,bhcl,bclhp->bchpn', bc, decay_states, xd)
    states = jnp.concatenate([h0[:, None], states], axis=1)
    chunk_decay = jnp.exp(segsum(jnp.pad(a_cs[..., -1], ((0, 0), (0, 0), (1, 0)))))
    states = jnp.einsum('bhzc,bchpn->bzhpn', chunk_decay, states)
    y_off = jnp.einsum('bclhn,bchpn,bhcl->bclhp', cc, states[:, :-1], jnp.exp(a_cs))
    y = (y_diag + y_off).reshape(b, L, SSD_HEADS, SSD_HEADDIM)
    return y, states[:, -1]


def ssd_mixer(z, xbc, dt_f, dt_b, lp, h0_f, h0_b):
    b, L, _ = xbc.shape
    xbc = jax.nn.silu(centred_dwconv(xbc, lp['conv_w'], lp['conv_b']))
    xs, bg, cg = jnp.split(xbc, [SSD_INNER, SSD_INNER + SSD_GROUPS * D_STATE], axis=-1)
    xs = xs.reshape(b, L, SSD_HEADS, SSD_HEADDIM)
    rep = SSD_HEADS // SSD_GROUPS
    bh = jnp.repeat(bg.reshape(b, L, SSD_GROUPS, D_STATE), rep, axis=2)
    ch = jnp.repeat(cg.reshape(b, L, SSD_GROUPS, D_STATE), rep, axis=2)
    a_heads = -jnp.exp(lp['a_log'])
    dtf = jax.nn.softplus(dt_f + lp['dt_bias'][0])
    dtb = jax.nn.softplus(dt_b + lp['dt_bias'][1])
    y_f, hf = ssd_scan(xs, dtf, a_heads[0], bh, ch, h0_f)
    y_b, hb = ssd_scan(jnp.flip(xs, 1), jnp.flip(dtb, 1), a_heads[1], jnp.flip(bh, 1), jnp.flip(ch, 1), h0_b)
    y = y_f + jnp.flip(y_b, 1) + lp['d_skip'][:, None] * xs
    y = y.reshape(b, L, SSD_INNER) * jax.nn.silu(z)
    return rms_norm(y, lp['ssd_norm']), hf, hb


def multiscale_pool(p, w_grp, scale):
    b, L, _ = p.shape
    pg = p.reshape(b, L, POOL_GROUPS, POOL_GROUP_DIM)
    cs = jnp.cumsum(jnp.pad(pg.astype(F32), ((0, 0), (1, 0), (0, 0), (0, 0))), axis=1)
    t = jnp.arange(L)
    outs = []
    for g, w in enumerate(POOL_WINDOWS):
        lo = jnp.clip(t - w // 2, 0, L)
        hi = jnp.clip(t + (w - w // 2), 0, L)
        cnt = (hi - lo).astype(F32)
        outs.append((cs[:, hi, g] - cs[:, lo, g]) / cnt[None, :, None])
    pooled = jnp.stack(outs, axis=2).astype(p.dtype)
    y = jnp.einsum('blgd,gde->blge', pooled - pg, w_grp).reshape(b, L, POOL_INNER)
    return y * scale


def mixers(h, lp, h0_f, h0_b, rope, ctx_k, ctx_v):
    b, L, _ = h.shape
    offsets = [int(o) for o in np.cumsum(IN_SPLITS)]
    z, xbc, dt_f, dt_b, q, k, v, pin, gate_logits = jnp.split(h @ lp['w_in'], offsets, axis=-1)
    y_ssd, hf, hb = ssd_mixer(z, xbc, dt_f, dt_b, lp, h0_f, h0_b)
    q = rms_norm(q.reshape(b, L, N_HEADS, HEAD_DIM), lp['q_norm'])
    k = rms_norm(k.reshape(b, L, KV_HEADS, HEAD_DIM), lp['k_norm'])
    v = v.reshape(b, L, KV_HEADS, HEAD_DIM)
    if rope is None:
        k_all, v_all = k, v
    else:
        q = apply_rope(q, rope[0], rope[1])
        k = apply_rope(k, rope[0], rope[1])
        k_all = jnp.concatenate([k, ctx_k], axis=1)
        v_all = jnp.concatenate([v, ctx_v], axis=1)
    o_attn = gqa_attend(q, k_all, v_all)
    o_pool = multiscale_pool(pin, lp['pool_w'], lp['pool_scale'])
    g = jax.nn.sigmoid(gate_logits.reshape(b, L, N_BRANCHES, D_MODEL))
    merged = (g[:, :, 0] * (y_ssd @ lp['w_ssd_out'])
              + g[:, :, 1] * (o_attn @ lp['w_attn_out'])
              + g[:, :, 2] * (o_pool @ lp['w_pool_out']))
    return merged @ lp['w_out'], k, v, hf, hb


def moe_ffn(h, w_router, b_router, w_gate, w_up, w_down):
    b, L, _ = h.shape
    scores = jax.nn.sigmoid(jnp.einsum('bld,de->ble', h, w_router, preferred_element_type=F32))
    biased = scores + b_router.astype(F32)
    grouped = biased.reshape(b, L, N_EXPERT_GROUPS, EXPERTS_PER_GROUP)
    group_score = jnp.sum(lax.top_k(grouped, TOP_K)[0], axis=-1)
    best = jnp.argmax(group_score, axis=-1)
    in_group = (jnp.arange(N_EXPERTS) // EXPERTS_PER_GROUP)[None, None, :] == best[..., None]
    _, idx = lax.top_k(jnp.where(in_group, biased, -jnp.inf), TOP_K)
    sel = jnp.take_along_axis(scores, idx, axis=-1)
    wts = sel / jnp.sum(sel, axis=-1, keepdims=True)
    gate = jnp.sum(jax.nn.one_hot(idx, N_EXPERTS, dtype=F32) * wts[..., None], axis=-2).astype(h.dtype)
    a = jnp.einsum('bld,edf->blef', h, w_gate)
    u = jnp.einsum('bld,edf->blef', h, w_up)
    act = jax.nn.silu(a) * u * gate[..., None]
    return jnp.einsum('blef,efd->bld', act, w_down)


def setup_inputs(seed: int = 0) -> dict:
    key = jax.random.key(seed)
    ks = iter(jax.random.split(key, 40))

    def nrm(shape, scale):
        return jax.random.normal(next(ks), shape, F32) * scale

    def gain(shape):
        return 1.0 + nrm(shape, 0.02)

    dt0 = jnp.exp(jax.random.uniform(next(ks), (DEPTH, 2, SSD_HEADS), F32, math.log(1e-3), math.log(1e-1)))
    dt_bias = dt0 + jnp.log(-jnp.expm1(-dt0))
    a_log = jnp.log(jax.random.uniform(next(ks), (DEPTH, 2, SSD_HEADS), F32, 1.0, 16.0))
    return {
        'x_prompt': nrm((BATCH, SEQ, D_MODEL), 1.0),
        'x_sample': nrm((DEC_BATCH, DEC_SEQ, D_MODEL), 1.0),
        'cache_k': nrm((DEC_BATCH, DEPTH, PAST_LEN, KV_HEADS, HEAD_DIM), 1.0),
        'cache_v': nrm((DEC_BATCH, DEPTH, PAST_LEN, KV_HEADS, HEAD_DIM), 1.0),
        'state_ssd_fwd': nrm((DEC_BATCH, DEPTH, SSD_HEADS, SSD_HEADDIM, D_STATE), 0.1),
        'state_ssd_bwd': nrm((DEC_BATCH, DEPTH, SSD_HEADS, SSD_HEADDIM, D_STATE), 0.1),
        'c': nrm((DEC_BATCH, D_MODEL), 1.0),
        'c_ctx': nrm((D_MODEL,), 1.0),
        'w_ada': nrm((DEPTH, D_MODEL, 6 * D_MODEL), 0.5 * D_MODEL ** -0.5),
        'b_ada': nrm((DEPTH, 6 * D_MODEL), 0.02),
        'norm_mix': gain((DEPTH, D_MODEL)),
        'norm_ffn': gain((DEPTH, D_MODEL)),
        'w_in': nrm((DEPTH, D_MODEL, D_IN), D_MODEL ** -0.5),
        'conv_w': nrm((DEPTH, CONV_W, XBC_DIM), CONV_W ** -0.5),
        'conv_b': nrm((DEPTH, XBC_DIM), 0.01),
        'a_log': a_log,
        'dt_bias': dt_bias,
        'd_skip': 1.0 + nrm((DEPTH, SSD_HEADS), 0.1),
        'ssd_norm': gain((DEPTH, SSD_INNER)),
        'q_norm': gain((DEPTH, HEAD_DIM)),
        'k_norm': gain((DEPTH, HEAD_DIM)),
        'pool_w': nrm((DEPTH, POOL_GROUPS, POOL_GROUP_DIM, POOL_GROUP_DIM), POOL_GROUP_DIM ** -0.5),
        'pool_scale': gain((DEPTH, POOL_INNER)),
        'w_ssd_out': nrm((DEPTH, SSD_INNER, D_MODEL), SSD_INNER ** -0.5),
        'w_attn_out': nrm((DEPTH, ATTN_INNER, D_MODEL), ATTN_INNER ** -0.5),
        'w_pool_out': nrm((DEPTH, POOL_INNER, D_MODEL), POOL_INNER ** -0.5),
        'w_out': nrm((DEPTH, D_MODEL, D_MODEL), D_MODEL ** -0.5),
        'w_router': nrm((D_MODEL, N_EXPERTS), D_MODEL ** -0.5),
        'b_router': nrm((N_EXPERTS,), 0.01),
        'w_gate_ff': nrm((DEPTH, N_EXPERTS, D_MODEL, D_FF), D_MODEL ** -0.5),
        'w_up_ff': nrm((DEPTH, N_EXPERTS, D_MODEL, D_FF), D_MODEL ** -0.5),
        'w_down_ff': nrm((DEPTH, N_EXPERTS, D_FF, D_MODEL), D_FF ** -0.5),
        'norm_final': gain((D_MODEL,)),
    }


def reference(x_prompt, x_sample, cache_k, cache_v, state_ssd_fwd, state_ssd_bwd, c, c_ctx,
              w_ada, b_ada, norm_mix, norm_ffn, w_in, conv_w, conv_b, a_log, dt_bias, d_skip, ssd_norm,
              q_norm, k_norm, pool_w, pool_scale, w_ssd_out, w_attn_out, w_pool_out, w_out,
              w_router, b_router, w_gate_ff, w_up_ff, w_down_ff, norm_final):
    bp = x_prompt.shape[0]
    ls = x_sample.shape[1]
    rope = axial_rope(ls)
    zero_state = jnp.zeros((bp, SSD_HEADS, SSD_HEADDIM, D_STATE), x_prompt.dtype)
    cond_ctx = jax.nn.silu(c_ctx)[None]
    cond_lat = jax.nn.silu(c)
    xc, xl = x_prompt, x_sample
    ks, vs, hfs, hbs = [], [], [], []
    for l in range(DEPTH):
        lp = {'w_in': w_in[l], 'conv_w': conv_w[l], 'conv_b': conv_b[l], 'a_log': a_log[l],
              'dt_bias': dt_bias[l], 'd_skip': d_skip[l], 'ssd_norm': ssd_norm[l], 'q_norm': q_norm[l],
              'k_norm': k_norm[l], 'pool_w': pool_w[l], 'pool_scale': pool_scale[l],
              'w_ssd_out': w_ssd_out[l], 'w_attn_out': w_attn_out[l], 'w_pool_out': w_pool_out[l],
              'w_out': w_out[l]}
        mc = jnp.split(cond_ctx @ w_ada[l] + b_ada[l], 6, axis=-1)
        ml = jnp.split(cond_lat @ w_ada[l] + b_ada[l], 6, axis=-1)
        mix, k_c, v_c, hf, hb = mixers(modulate(xc, norm_mix[l], mc[0], mc[1]), lp, zero_state, zero_state,
                                       None, None, None)
        xc = xc + mc[2][:, None] * mix
        xc = xc + mc[5][:, None] * moe_ffn(modulate(xc, norm_ffn[l], mc[3], mc[4]), w_router, b_router,
                                           w_gate_ff[l], w_up_ff[l], w_down_ff[l])
        ks.append(k_c)
        vs.append(v_c)
        hfs.append(hf)
        hbs.append(hb)
        mix, _, _, _, _ = mixers(modulate(xl, norm_mix[l], ml[0], ml[1]), lp, state_ssd_fwd[:, l],
                                 state_ssd_bwd[:, l], rope, cache_k[:, l], cache_v[:, l])
        xl = xl + ml[2][:, None] * mix
        xl = xl + ml[5][:, None] * moe_ffn(modulate(xl, norm_ffn[l], ml[3], ml[4]), w_router, b_router,
                                           w_gate_ff[l], w_up_ff[l], w_down_ff[l])
    y_prompt = rms_norm(xc, norm_final)
    y_sample = rms_norm(xl, norm_final)
    new_cache_k = jnp.stack(ks, axis=1)
    new_cache_v = jnp.stack(vs, axis=1)
    new_state_ssd_fwd = jnp.stack(hfs, axis=1)
    new_state_ssd_bwd = jnp.stack(hbs, axis=1)
    return (y_prompt, y_sample, new_cache_k, new_cache_v, new_state_ssd_fwd, new_state_ssd_bwd)
```

```python
import functools

import jax
import jax.numpy as jnp
from jax import lax
from jax.experimental import pallas as pl
from jax.experimental.pallas import tpu as pltpu

F32 = jnp.float32
BF16 = jnp.bfloat16
I32 = jnp.int32

D_MODEL = 2048
GRID_W = 64
EPS = 1e-6

SSD_HEADS = 16
SSD_HEADDIM = 64
SSD_INNER = SSD_HEADS * SSD_HEADDIM
SSD_GROUPS = 4
D_STATE = 64
CONV_W = 5
CHUNK = 128
XBC_DIM = SSD_INNER + 2 * SSD_GROUPS * D_STATE

N_HEADS = 16
KV_HEADS = 4
HEAD_DIM = 64
ATTN_INNER = N_HEADS * HEAD_DIM
KV_INNER = KV_HEADS * HEAD_DIM
ROPE_THETA = 10000.0

POOL_GROUPS = 4
POOL_INNER = 1024
POOL_GROUP_DIM = POOL_INNER // POOL_GROUPS
POOL_WINDOWS = (2, 4, 8, 16)

N_EXPERTS = 16
N_EXPERT_GROUPS = 4
EXPERTS_PER_GROUP = N_EXPERTS // N_EXPERT_GROUPS
D_FF = 512

_O_Z = 0
_O_XBC = _O_Z + SSD_INNER
_O_DTF = _O_XBC + XBC_DIM
_O_DTB = _O_DTF + SSD_HEADS
_O_Q = _O_DTB + SSD_HEADS
_O_K = _O_Q + ATTN_INNER
_O_V = _O_K + KV_INNER
_O_PIN = _O_V + KV_INNER
_O_GATE = _O_PIN + POOL_INNER

MAIN_N = 3 * D_MODEL + SSD_INNER + ATTN_INNER + POOL_INNER + XBC_DIM
LANES = 128
SMALL_N = 2 * KV_INNER + LANES

HALO = 16
MOE_TILE = 256
VMEM_LIMIT = 56 * 1024 * 1024


def _cparams(*sem):
    return pltpu.CompilerParams(dimension_semantics=sem, vmem_limit_bytes=VMEM_LIMIT)


def _silu(x):
    return x * jax.nn.sigmoid(x)


def _rms(xf):
    return xf * lax.rsqrt(jnp.mean(xf * xf, axis=-1, keepdims=True) + EPS)


def _split_bf16(x):
    hi = x.astype(BF16)
    lo = (x - hi.astype(F32)).astype(BF16)
    return hi, lo


def _ada_kernel(cond_ref, w_ref, b_ref, o_ref):
    c = _silu(cond_ref[...])
    o_ref[...] = jnp.dot(c.astype(BF16), w_ref[...].astype(BF16), preferred_element_type=F32) + b_ref[...]


def ada_mods(cond, w_ada, b_ada, tn=1024):
    rows = cond.shape[0]
    depth, _, n = w_ada.shape
    return pl.pallas_call(
        _ada_kernel,
        grid=(depth, n // tn),
        in_specs=[
            pl.BlockSpec((rows, D_MODEL), lambda l, j: (0, 0)),
            pl.BlockSpec((None, D_MODEL, tn), lambda l, j: (l, 0, j)),
            pl.BlockSpec((None, 1, tn), lambda l, j: (l, 0, j)),
        ],
        out_specs=pl.BlockSpec((None, rows, tn), lambda l, j: (l, 0, j)),
        out_shape=jax.ShapeDtypeStruct((depth, rows, n), F32),
        compiler_params=_cparams("parallel", "parallel"),
        name="ada_mods",
    )(cond, w_ada, b_ada.reshape(depth, 1, n))


def _mod_spec(which, row_of_tile):
    return pl.BlockSpec((None, None, 1, D_MODEL), lambda i: (row_of_tile(i), which, 0, 0))


def _row_fn(path, tm):
    row0, per_seq, seq_len = path
    if per_seq:
        return lambda i: row0 + (i * tm) // seq_len
    return lambda i: row0


def _modulate_kernel(x_ref, g_ref, sh_ref, sc_ref, h_ref):
    xf = x_ref[...]
    h_ref[...] = (_rms(xf) * g_ref[...] * (1.0 + sc_ref[...]) + sh_ref[...]).astype(h_ref.dtype)


def modulate_first(x, mods, gain, path, tm=512):
    t = x.shape[0]
    rf = _row_fn(path, tm)
    return pl.pallas_call(
        _modulate_kernel,
        grid=(t // tm,),
        in_specs=[
            pl.BlockSpec((tm, D_MODEL), lambda i: (i, 0)),
            pl.BlockSpec((1, D_MODEL), lambda i: (0, 0)),
            _mod_spec(0, rf),
            _mod_spec(1, rf),
        ],
        out_specs=pl.BlockSpec((tm, D_MODEL), lambda i: (i, 0)),
        out_shape=jax.ShapeDtypeStruct((t, D_MODEL), BF16),
        compiler_params=_cparams("parallel"),
        name="modulate_first",
    )(x, gain, mods, mods)


def _residual_modulate_kernel(x_ref, ya_ref, yb_ref, g2_ref, gain_ref, sh_ref, sc_ref, xo_ref, h_ref):
    xn = x_ref[...] + g2_ref[...] * (ya_ref[...].astype(F32) + yb_ref[...].astype(F32))
    xo_ref[...] = xn
    h_ref[...] = (_rms(xn) * gain_ref[...] * (1.0 + sc_ref[...]) + sh_ref[...]).astype(h_ref.dtype)


def residual_modulate(x, y2, mods_prev, mods_next, gain_next, path, tm=512):
    t = x.shape[0]
    nt = t // tm
    rf = _row_fn(path, tm)
    return pl.pallas_call(
        _residual_modulate_kernel,
        grid=(nt,),
        in_specs=[
            pl.BlockSpec((tm, D_MODEL), lambda i: (i, 0)),
            pl.BlockSpec((tm, D_MODEL), lambda i: (i, 0)),
            pl.BlockSpec((tm, D_MODEL), lambda i: (i + nt, 0)),
            _mod_spec(5, rf),
            pl.BlockSpec((1, D_MODEL), lambda i: (0, 0)),
            _mod_spec(0, rf),
            _mod_spec(1, rf),
        ],
        out_specs=[
            pl.BlockSpec((tm, D_MODEL), lambda i: (i, 0)),
            pl.BlockSpec((tm, D_MODEL), lambda i: (i, 0)),
        ],
        out_shape=[
            jax.ShapeDtypeStruct((t, D_MODEL), F32),
            jax.ShapeDtypeStruct((t, D_MODEL), BF16),
        ],
        compiler_params=_cparams("parallel"),
        name="residual_modulate",
    )(x, y2, y2, mods_prev, gain_next, mods_next, mods_next)


def _residual_final_kernel(x_ref, ya_ref, yb_ref, g2_ref, gain_ref, y_ref):
    xn = x_ref[...] + g2_ref[...] * (ya_ref[...].astype(F32) + yb_ref[...].astype(F32))
    y_ref[...] = _rms(xn) * gain_ref[...]


def residual_final(x, y2, mods_prev, gain, path, tm=512):
    t = x.shape[0]
    nt = t // tm
    rf = _row_fn(path, tm)
    return pl.pallas_call(
        _residual_final_kernel,
        grid=(nt,),
        in_specs=[
            pl.BlockSpec((tm, D_MODEL), lambda i: (i, 0)),
            pl.BlockSpec((tm, D_MODEL), lambda i: (i, 0)),
            pl.BlockSpec((tm, D_MODEL), lambda i: (i + nt, 0)),
            _mod_spec(5, rf),
            pl.BlockSpec((1, D_MODEL), lambda i: (0, 0)),
        ],
        out_specs=pl.BlockSpec((tm, D_MODEL), lambda i: (i, 0)),
        out_shape=jax.ShapeDtypeStruct((t, D_MODEL), F32),
        compiler_params=_cparams("parallel"),
        name="residual_final",
    )(x, y2, y2, mods_prev, gain)


def _mm_kernel(a_ref, b_ref, o_ref):
    o_ref[...] = jnp.dot(a_ref[...], b_ref[...], preferred_element_type=F32).astype(o_ref.dtype)


def matmul(a, b, out_dtype, tm, tn, name):
    m, k = a.shape
    n = b.shape[1]
    return pl.pallas_call(
        _mm_kernel,
        grid=(m // tm, n // tn),
        in_specs=[
            pl.BlockSpec((tm, k), lambda i, j: (i, 0)),
            pl.BlockSpec((k, tn), lambda i, j: (0, j)),
        ],
        out_specs=pl.BlockSpec((tm, tn), lambda i, j: (i, j)),
        out_shape=jax.ShapeDtypeStruct((m, n), out_dtype),
        compiler_params=_cparams("parallel", "arbitrary"),
        name=name,
    )(a, b)


def _chunk_window(ref, c, n_chunks, seq_len):
    s = pl.multiple_of(c * CHUNK, CHUNK)
    cur = ref[pl.ds(s, CHUNK), :].astype(F32)
    sp = pl.multiple_of(jnp.maximum(s - HALO, 0), HALO)
    sn = pl.multiple_of(jnp.minimum(s + CHUNK, seq_len - HALO), HALO)
    prev = ref[pl.ds(sp, HALO), :].astype(F32)
    nxt = ref[pl.ds(sn, HALO), :].astype(F32)
    prev = jnp.where(c > 0, prev, 0.0)
    nxt = jnp.where(c < n_chunks - 1, nxt, 0.0)
    return jnp.concatenate([prev, cur, nxt], axis=0)


def _ssd_kernel(*refs, seq_len, has_h0):
    if has_h0:
        (xbc_ref, z_ref, dt_ref, h0f_ref, h0b_ref, cw_ref, cb_ref, dtb_ref, alog_ref, dskip_ref, gain_ref,
         y_ref, hf_ref, hb_ref, xc_ref, sp_ref, yacc_ref, st_ref) = refs
    else:
        (xbc_ref, z_ref, dt_ref, cw_ref, cb_ref, dtb_ref, alog_ref, dskip_ref, gain_ref,
         y_ref, hf_ref, hb_ref, xc_ref, sp_ref, yacc_ref, st_ref) = refs
        h0f_ref = h0b_ref = None
    n_chunks = seq_len // CHUNK
    win = CHUNK + 2 * HALO

    def conv_chunk(c, carry):
        w = _chunk_window(xbc_ref, c, n_chunks, seq_len)
        acc = jnp.zeros((CHUNK, XBC_DIM), F32) + cb_ref[...]
        for j in range(CONV_W):
            shift = (CONV_W // 2 - j) % win
            wj = w if shift == 0 else pltpu.roll(w, shift, 0)
            acc = acc + cw_ref[j:j + 1, :] * wj[HALO:HALO + CHUNK, :]
        s = pl.multiple_of(c * CHUNK, CHUNK)
        xc_ref[pl.ds(s, CHUNK), :] = _silu(acc).astype(BF16)
        return carry

    lax.fori_loop(0, n_chunks, conv_chunk, 0)

    xdt = dt_ref[...] + dtb_ref[...]
    sp_ref[...] = jnp.maximum(xdt, 0.0) + jnp.log1p(jnp.exp(-jnp.abs(xdt)))
    a_row = -jnp.exp(alog_ref[...])

    rows = lax.broadcasted_iota(I32, (CHUNK, CHUNK), 0)
    cols = lax.broadcasted_iota(I32, (CHUNK, CHUNK), 1)
    lane_lo = cols < SSD_HEADDIM
    row_lo = rows < SSD_HEADDIM

    def pair_cols(arr, lane):
        return jnp.where(lane_lo, arr[:, lane:lane + 1], arr[:, lane + 1:lane + 2])

    def scan_chunk(c, forward):
        off = 0 if forward else SSD_HEADS
        s = pl.multiple_of(c * CHUNK, CHUNK)
        xcv = xc_ref[pl.ds(s, CHUNK), :]
        dtc = sp_ref[pl.ds(s, CHUNK), :]
        a = dtc * a_row
        tri = (rows >= cols) if forward else (cols >= rows)
        tri_b = jnp.where(tri, 1.0, 0.0).astype(BF16)
        a1 = a.astype(BF16)
        r1 = a - a1.astype(F32)
        a2 = r1.astype(BF16)
        a3 = (r1 - a2.astype(F32)).astype(BF16)
        cs = (jnp.dot(tri_b, a1, preferred_element_type=F32) + jnp.dot(tri_b, a2, preferred_element_type=F32)
              + jnp.dot(tri_b, a3, preferred_element_type=F32))
        cs_t = cs.T
        tot = cs[CHUNK - 1:CHUNK, :] if forward else cs[0:1, :]
        decay_out = jnp.exp(cs)
        decay_state = jnp.exp(tot - cs)
        e_tot = jnp.exp(tot)
        ys = []
        for g in range(SSD_GROUPS):
            bg = xcv[:, SSD_INNER + g * D_STATE:SSD_INNER + (g + 1) * D_STATE]
            cg = xcv[:, SSD_INNER + (SSD_GROUPS + g) * D_STATE:SSD_INNER + (SSD_GROUPS + g + 1) * D_STATE]
            gmat = lax.dot_general(cg, bg, (((1,), (1,)), ((), ())), preferred_element_type=F32)
            for pp in range(2):
                j = 2 * g + pp
                lane = off + 2 * j
                ms = []
                for hh in range(2):
                    diff = cs[:, lane + hh:lane + hh + 1] - cs_t[lane + hh:lane + hh + 1, :]
                    lm = jnp.exp(jnp.where(tri, diff, -jnp.inf))
                    ms.append((gmat * lm).astype(BF16))
                mcat = jnp.concatenate(ms, axis=1)
                xp = xcv[:, j * CHUNK:(j + 1) * CHUNK].astype(F32)
                xd = xp * pair_cols(dtc, lane)
                xdb = xd.astype(BF16)
                zero = jnp.zeros_like(xdb)
                xblk = jnp.concatenate([jnp.where(lane_lo, xdb, zero), jnp.where(lane_lo, zero, xdb)], axis=0)
                y_diag = jnp.dot(mcat, xblk, preferred_element_type=F32)
                st = st_ref[j]
                y_off = lax.dot_general(cg, st.astype(BF16), (((1,), (1,)), ((), ())),
                                        preferred_element_type=F32) * pair_cols(decay_out, lane)
                xdd = (xd * pair_cols(decay_state, lane)).T.astype(BF16)
                contrib = jnp.dot(xdd, bg, preferred_element_type=F32)
                e_pair = jnp.where(row_lo[:, 0:1], e_tot[:, lane:lane + 1], e_tot[:, lane + 1:lane + 2])
                st_ref[j] = st * e_pair + contrib
                ys.append(y_diag + y_off)
        return s, xcv, jnp.concatenate(ys, axis=1)

    def fwd_chunk(c, carry):
        s, _, y = scan_chunk(c, True)
        yacc_ref[pl.ds(s, CHUNK), :] = y
        return carry

    def bwd_chunk(i, carry):
        c = n_chunks - 1 - i
        s, xcv, y = scan_chunk(c, False)
        xs = xcv[:, :SSD_INNER].astype(F32)
        y = yacc_ref[pl.ds(s, CHUNK), :] + y + dskip_ref[...] * xs
        y = y * _silu(z_ref[pl.ds(s, CHUNK), :].astype(F32))
        y_ref[pl.ds(s, CHUNK), :] = (_rms(y) * gain_ref[...]).astype(y_ref.dtype)
        return carry

    if has_h0:
        st_ref[...] = h0f_ref[...]
    else:
        st_ref[...] = jnp.zeros_like(st_ref)
    lax.fori_loop(0, n_chunks, fwd_chunk, 0)
    hf_ref[...] = st_ref[...]
    if has_h0:
        st_ref[...] = h0b_ref[...]
    else:
        st_ref[...] = jnp.zeros_like(st_ref)
    lax.fori_loop(0, n_chunks, bwd_chunk, 0)
    hb_ref[...] = st_ref[...]


def ssd_mixer(main, small, h0f, h0b, lw, n_seq, seq_len):
    t = n_seq * seq_len
    has_h0 = h0f is not None
    pairs = SSD_HEADS // 2
    st_shape = (pairs, 2 * SSD_HEADDIM, D_STATE)
    st_spec = pl.BlockSpec((None,) + st_shape, lambda b: (b, 0, 0, 0))
    full = lambda shape: pl.BlockSpec(shape, lambda b: (0,) * len(shape))
    in_specs = [
        pl.BlockSpec((seq_len, XBC_DIM), lambda b: (b, (MAIN_N - XBC_DIM) // XBC_DIM)),
        pl.BlockSpec((seq_len, SSD_INNER), lambda b: (b, 3 * D_MODEL // SSD_INNER)),
        pl.BlockSpec((seq_len, LANES), lambda b: (b, 2 * KV_INNER // LANES)),
    ]
    args = [main, main, small]
    if has_h0:
        in_specs += [st_spec, st_spec]
        args += [h0f, h0b]
    in_specs += [full((8, XBC_DIM)), full((1, XBC_DIM)), full((1, LANES)), full((1, LANES)),
                 full((1, SSD_INNER)), full((1, SSD_INNER))]
    args += [lw["conv_w"], lw["conv_b"], lw["dt_bias"], lw["a_log"], lw["d_skip"], lw["ssd_norm"]]
    return pl.pallas_call(
        functools.partial(_ssd_kernel, seq_len=seq_len, has_h0=has_h0),
        grid=(n_seq,),
        in_specs=in_specs,
        out_specs=[pl.BlockSpec((seq_len, SSD_INNER), lambda b: (b, 0)), st_spec, st_spec],
        out_shape=[
            jax.ShapeDtypeStruct((t, SSD_INNER), BF16),
            jax.ShapeDtypeStruct((n_seq,) + st_shape, F32),
            jax.ShapeDtypeStruct((n_seq,) + st_shape, F32),
        ],
        scratch_shapes=[
            pltpu.VMEM((seq_len, XBC_DIM), BF16),
            pltpu.VMEM((seq_len, LANES), F32),
            pltpu.VMEM((seq_len, SSD_INNER), F32),
            pltpu.VMEM(st_shape, F32),
        ],
        compiler_params=_cparams("parallel"),
        name="ssd_mixer",
    )(*args)


def _swap_pairs(x):
    n = x.shape[-1]
    lane = lax.broadcasted_iota(I32, x.shape, x.ndim - 1)
    return jnp.where(lane % 2 == 0, pltpu.roll(x, n - 1, x.ndim - 1), pltpu.roll(x, 1, x.ndim - 1))


def _kv_prep_kernel(*refs, seq_len, rope):
    if rope:
        (k_ref, v_ref, gain_ref, cos_ref, sin_ref, ck_ref, cv_ref, kt_ref, v4_ref) = refs
    else:
        (k_ref, v_ref, gain_ref, kn_ref, kt_ref, v4_ref) = refs
    kf = k_ref[...]
    parts = []
    for g in range(KV_HEADS):
        kh = kf[:, g * HEAD_DIM:(g + 1) * HEAD_DIM]
        parts.append(kh * lax.rsqrt(jnp.mean(kh * kh, axis=-1, keepdims=True) + EPS))
    kn = jnp.concatenate(parts, axis=1) * gain_ref[...]
    if rope:
        kn = kn * cos_ref[...] + _swap_pairs(kn) * sin_ref[...]
    else:
        kn_ref[...] = kn
    kt_ref[:, 0:seq_len] = kn.T.astype(BF16)
    vf = v_ref[...]
    for g in range(KV_HEADS):
        v4_ref[g, 0:seq_len, :] = vf[:, g * HEAD_DIM:(g + 1) * HEAD_DIM].astype(BF16)
    if rope:
        past = ck_ref.shape[0]
        kt_ref[:, seq_len:seq_len + past] = ck_ref[...].T.astype(BF16)
        cvf = cv_ref[...]
        for g in range(KV_HEADS):
            v4_ref[g, seq_len:seq_len + past, :] = cvf[:, g * HEAD_DIM:(g + 1) * HEAD_DIM].astype(BF16)


def kv_prep(small, k_gain, n_seq, seq_len, rope_kv=None, cache=None):
    t = n_seq * seq_len
    rope = rope_kv is not None
    in_specs = [
        pl.BlockSpec((seq_len, KV_INNER), lambda b: (b, 0)),
        pl.BlockSpec((seq_len, KV_INNER), lambda b: (b, 1)),
        pl.BlockSpec((1, KV_INNER), lambda b: (0, 0)),
    ]
    args = [small, small, k_gain]
    n_keys = seq_len
    out_specs, out_shape = [], []
    if rope:
        cache_k, cache_v, layer = cache
        past = cache_k.shape[2]
        n_keys += past
        in_specs += [
            pl.BlockSpec((seq_len, KV_INNER), lambda b: (0, 0)),
            pl.BlockSpec((seq_len, KV_INNER), lambda b: (0, 0)),
            pl.BlockSpec((None, None, past, KV_INNER), lambda b: (b, layer, 0, 0)),
            pl.BlockSpec((None, None, past, KV_INNER), lambda b: (b, layer, 0, 0)),
        ]
        args += [rope_kv[0], rope_kv[1], cache_k, cache_v]
    else:
        out_specs.append(pl.BlockSpec((seq_len, KV_INNER), lambda b: (b, 0)))
        out_shape.append(jax.ShapeDtypeStruct((t, KV_INNER), F32))
    out_specs += [
        pl.BlockSpec((None, KV_INNER, n_keys), lambda b: (b, 0, 0)),
        pl.BlockSpec((None, KV_HEADS, n_keys, HEAD_DIM), lambda b: (b, 0, 0, 0)),
    ]
    out_shape += [
        jax.ShapeDtypeStruct((n_seq, KV_INNER, n_keys), BF16),
        jax.ShapeDtypeStruct((n_seq, KV_HEADS, n_keys, HEAD_DIM), BF16),
    ]
    return pl.pallas_call(
        functools.partial(_kv_prep_kernel, seq_len=seq_len, rope=rope),
        grid=(n_seq,),
        in_specs=in_specs,
        out_specs=out_specs,
        out_shape=out_shape,
        compiler_params=_cparams("parallel"),
        name="kv_prep",
    )(*args)


def _attn_kernel(*refs, rope, tq):
    if rope:
        q_ref, gain_ref, cos_ref, sin_ref, kt_ref, v4_ref, o_ref = refs
    else:
        q_ref, gain_ref, kt_ref, v4_ref, o_ref = refs
    rep = N_HEADS // KV_HEADS
    scale = HEAD_DIM ** -0.5
    heads = []
    for j in range(N_HEADS // 2):
        qs = q_ref[:, j * LANES:(j + 1) * LANES].astype(F32)
        qg = qs * gain_ref[...]
        if rope:
            qg = qg * cos_ref[...] + _swap_pairs(qg) * sin_ref[...]
        for hh in range(2):
            raw = qs[:, hh * HEAD_DIM:(hh + 1) * HEAD_DIM]
            inv = lax.rsqrt(jnp.mean(raw * raw, axis=-1, keepdims=True) + EPS) * scale
            heads.append((qg[:, hh * HEAD_DIM:(hh + 1) * HEAD_DIM] * inv).astype(BF16))
    for g in range(KV_HEADS):
        qstack = jnp.concatenate(heads[g * rep:(g + 1) * rep], axis=0)
        kt = kt_ref[g * HEAD_DIM:(g + 1) * HEAD_DIM, :]
        s = jnp.dot(qstack, kt, preferred_element_type=F32)
        m = jnp.max(s, axis=-1, keepdims=True)
        p = jnp.exp(s - m)
        denom = jnp.sum(p, axis=-1, keepdims=True)
        o = jnp.dot(p.astype(BF16), v4_ref[g], preferred_element_type=F32) / denom
        for r in range(rep):
            h = g * rep + r
            o_ref[:, h * HEAD_DIM:(h + 1) * HEAD_DIM] = o[r * tq:(r + 1) * tq, :].astype(o_ref.dtype)


def attention(main, q_gain, kt, v4, n_seq, seq_len, tq, rope_q=None):
    t = n_seq * seq_len
    nq = seq_len // tq
    n_keys = kt.shape[2]
    rope = rope_q is not None
    in_specs = [
        pl.BlockSpec((tq, ATTN_INNER), lambda b, i: (b * nq + i, (3 * D_MODEL + SSD_INNER) // ATTN_INNER)),
        pl.BlockSpec((1, LANES), lambda b, i: (0, 0)),
    ]
    args = [main, q_gain]
    if rope:
        in_specs += [pl.BlockSpec((tq, LANES), lambda b, i: (i, 0)), pl.BlockSpec((tq, LANES), lambda b, i: (i, 0))]
        args += [rope_q[0], rope_q[1]]
    in_specs += [
        pl.BlockSpec((None, KV_INNER, n_keys), lambda b, i: (b, 0, 0)),
        pl.BlockSpec((None, KV_HEADS, n_keys, HEAD_DIM), lambda b, i: (b, 0, 0, 0)),
    ]
    args += [kt, v4]
    return pl.pallas_call(
        functools.partial(_attn_kernel, rope=rope, tq=tq),
        grid=(n_seq, nq),
        in_specs=in_specs,
        out_specs=pl.BlockSpec((tq, ATTN_INNER), lambda b, i: (b * nq + i, 0)),
        out_shape=jax.ShapeDtypeStruct((t, ATTN_INNER), BF16),
        compiler_params=_cparams("parallel", "arbitrary"),
        name="attention",
    )(*args)


def _pool_kernel(p_ref, w_ref, scale_ref, o_ref, *, seq_len):
    n_chunks = seq_len // CHUNK
    win = CHUNK + 2 * HALO
    t_loc = lax.broadcasted_iota(I32, (CHUNK, win), 0)
    r_loc = lax.broadcasted_iota(I32, (CHUNK, win), 1) - HALO
    t_col = lax.broadcasted_iota(I32, (CHUNK, 1), 0)

    def chunk(c, carry):
        s = pl.multiple_of(c * CHUNK, CHUNK)
        w = _chunk_window(p_ref, c, n_chunks, seq_len).astype(BF16)
        outs = []
        for g, width in enumerate(POOL_WINDOWS):
            half = width // 2
            d = r_loc - t_loc
            band = jnp.where((d >= -half) & (d < width - half), 1.0, 0.0).astype(BF16)
            wg = w[:, g * POOL_GROUP_DIM:(g + 1) * POOL_GROUP_DIM]
            sums = jnp.dot(band, wg, preferred_element_type=F32)
            tg = t_col + s
            cnt = (jnp.minimum(tg + (width - half), seq_len) - jnp.maximum(tg - half, 0)).astype(F32)
            cur = wg[HALO:HALO + CHUNK, :].astype(F32)
            diff = sums / cnt - cur
            outs.append(jnp.dot(diff.astype(BF16), w_ref[g], preferred_element_type=F32))
        o_ref[pl.ds(s, CHUNK), :] = (jnp.concatenate(outs, axis=1) * scale_ref[...]).astype(o_ref.dtype)
        return carry

    lax.fori_loop(0, n_chunks, chunk, 0)


def pool_mixer(main, pool_w, pool_scale, n_seq, seq_len):
    t = n_seq * seq_len
    return pl.pallas_call(
        functools.partial(_pool_kernel, seq_len=seq_len),
        grid=(n_seq,),
        in_specs=[
            pl.BlockSpec((seq_len, POOL_INNER), lambda b: (b, (3 * D_MODEL + SSD_INNER + ATTN_INNER) // POOL_INNER)),
            pl.BlockSpec((POOL_GROUPS, POOL_GROUP_DIM, POOL_GROUP_DIM), lambda b: (0, 0, 0)),
            pl.BlockSpec((1, POOL_INNER), lambda b: (0, 0)),
        ],
        out_specs=pl.BlockSpec((seq_len, POOL_INNER), lambda b: (b, 0)),
        out_shape=jax.ShapeDtypeStruct((t, POOL_INNER), BF16),
        compiler_params=_cparams("parallel"),
        name="pool_mixer",
    )(main, pool_w, pool_scale)


def _merge_kernel(ys_ref, ya_ref, yp_ref, g0_ref, g1_ref, g2_ref, ws_ref, wa_ref, wp_ref, o_ref):
    acc = jax.nn.sigmoid(g0_ref[...].astype(F32)) * jnp.dot(ys_ref[...], ws_ref[...], preferred_element_type=F32)
    acc = acc + jax.nn.sigmoid(g1_ref[...].astype(F32)) * jnp.dot(ya_ref[...], wa_ref[...],
                                                                  preferred_element_type=F32)
    acc = acc + jax.nn.sigmoid(g2_ref[...].astype(F32)) * jnp.dot(yp_ref[...], wp_ref[...],
                                                                  preferred_element_type=F32)
    o_ref[...] = acc.astype(o_ref.dtype)


def merge_branches(y_ssd, o_attn, o_pool, main, lw, tm=512):
    t = y_ssd.shape[0]
    act = lambda: pl.BlockSpec((tm, SSD_INNER), lambda i: (i, 0))
    gate = lambda k: pl.BlockSpec((tm, D_MODEL), lambda i: (i, k))
    wspec = lambda: pl.BlockSpec((SSD_INNER, D_MODEL), lambda i: (0, 0))
    return pl.pallas_call(
        _merge_kernel,
        grid=(t // tm,),
        in_specs=[act(), act(), act(), gate(0), gate(1), gate(2), wspec(), wspec(), wspec()],
        out_specs=pl.BlockSpec((tm, D_MODEL), lambda i: (i, 0)),
        out_shape=jax.ShapeDtypeStruct((t, D_MODEL), BF16),
        compiler_params=_cparams("parallel"),
        name="merge_branches",
    )(y_ssd, o_attn, o_pool, main, main, main, lw["w_ssd_out"], lw["w_attn_out"], lw["w_pool_out"])


def _route(logits_t, bias_col):
    scores = jax.nn.sigmoid(logits_t)
    biased = scores + bias_col
    sc = [scores[e:e + 1, :] for e in range(N_EXPERTS)]
    bi = [biased[e:e + 1, :] for e in range(N_EXPERTS)]
    group_scores = []
    for g in range(N_EXPERT_GROUPS):
        a, b, c, d = bi[4 * g:4 * g + 4]
        hi1, lo1 = jnp.maximum(a, b), jnp.minimum(a, b)
        hi2, lo2 = jnp.maximum(c, d), jnp.minimum(c, d)
        top1 = jnp.maximum(hi1, hi2)
        top2 = jnp.maximum(jnp.minimum(hi1, hi2), jnp.maximum(lo1, lo2))
        group_scores.append(top1 + top2)
    best = jnp.zeros_like(group_scores[0], dtype=I32)
    best_v = group_scores[0]
    for g in range(1, N_EXPERT_GROUPS):
        upd = group_scores[g] > best_v
        best = jnp.where(upd, g, best)
        best_v = jnp.where(upd, group_scores[g], best_v)
    vb, vs = [], []
    for j in range(EXPERTS_PER_GROUP):
        b_j, s_j = bi[j], sc[j]
        for g in range(1, N_EXPERT_GROUPS):
            sel = best == g
            b_j = jnp.where(sel, bi[4 * g + j], b_j)
            s_j = jnp.where(sel, sc[4 * g + j], s_j)
        vb.append(b_j)
        vs.append(s_j)

    def first_argmax(vals, excluded=None):
        idx = None
        for j in range(EXPERTS_PER_GROUP):
            v = vals[j] if excluded is None else jnp.where(excluded == j, -jnp.inf, vals[j])
            if idx is None:
                idx, cur = jnp.zeros_like(best), v
            else:
                upd = v > cur
                idx = jnp.where(upd, j, idx)
                cur = jnp.where(upd, v, cur)
        return idx

    i0 = first_argmax(vb)
    i1 = first_argmax(vb, excluded=i0)

    def pick(vals, idx):
        out = vals[0]
        for j in range(1, EXPERTS_PER_GROUP):
            out = jnp.where(idx == j, vals[j], out)
        return out

    s0, s1 = pick(vs, i0), pick(vs, i1)
    tot = s0 + s1
    ids = jnp.concatenate([best * EXPERTS_PER_GROUP + i0, best * EXPERTS_PER_GROUP + i1], axis=0)
    wts = jnp.concatenate([s0 / tot, s1 / tot], axis=0)
    return ids, wts


def _outproj_kernel(x_ref, m_ref, wo_ref, g1_ref, gain_ref, sh_ref, sc_ref, wr_ref, wr_hi_ref, br_ref,
                    xo_ref, h_ref, ids_ref, wts_ref):
    mix = jnp.dot(m_ref[...], wo_ref[...], preferred_element_type=F32)
    xn = x_ref[...] + g1_ref[...] * mix
    xo_ref[...] = xn
    h = _rms(xn) * gain_ref[...] * (1.0 + sc_ref[...]) + sh_ref[...]
    h_hi, h_lo = _split_bf16(h)
    h_ref[...] = h_hi
    lg = jnp.dot(h_hi, wr_ref[...], preferred_element_type=F32) + jnp.dot(h_lo, wr_hi_ref[...],
                                                                         preferred_element_type=F32)
    lg_t = lg.T
    logits_t = lg_t[0:N_EXPERTS, :] + lg_t[N_EXPERTS:2 * N_EXPERTS, :]
    ids, wts = _route(logits_t, br_ref[...])
    ids_ref[...] = ids
    wts_ref[...] = wts


def outproj_route(x, merged, mods, lw, gw, path, tm=512):
    t = x.shape[0]
    rf = _row_fn(path, tm)
    row = lambda: pl.BlockSpec((1, D_MODEL), lambda i: (0, 0))
    return pl.pallas_call(
        _outproj_kernel,
        grid=(t // tm,),
        in_specs=[
            pl.BlockSpec((tm, D_MODEL), lambda i: (i, 0)),
            pl.BlockSpec((tm, D_MODEL), lambda i: (i, 0)),
            pl.BlockSpec((D_MODEL, D_MODEL), lambda i: (0, 0)),
            _mod_spec(2, rf),
            row(),
            _mod_spec(3, rf),
            _mod_spec(4, rf),
            pl.BlockSpec((D_MODEL, LANES), lambda i: (0, 0)),
            pl.BlockSpec((D_MODEL, LANES), lambda i: (0, 0)),
            pl.BlockSpec((N_EXPERTS, 1), lambda i: (0, 0)),
        ],
        out_specs=[
            pl.BlockSpec((tm, D_MODEL), lambda i: (i, 0)),
            pl.BlockSpec((tm, D_MODEL), lambda i: (i, 0)),
            pl.BlockSpec((2, tm), lambda i: (0, i)),
            pl.BlockSpec((2, tm), lambda i: (0, i)),
        ],
        out_shape=[
            jax.ShapeDtypeStruct((t, D_MODEL), F32),
            jax.ShapeDtypeStruct((t, D_MODEL), BF16),
            jax.ShapeDtypeStruct((2, t), I32),
            jax.ShapeDtypeStruct((2, t), F32),
        ],
        compiler_params=_cparams("parallel"),
        name="outproj_route",
    )(x, merged, lw["w_out"], mods, lw["norm_ffn"], mods, mods, gw["w_router_cat"], gw["w_router_hi"],
      gw["b_router"])


def _moe_kernel(tile_expert_ref, n_used_ref, x_ref, wt_ref, wg_ref, wu_ref, wd_ref, o_ref):
    i = pl.program_id(0)

    @pl.when(i < n_used_ref[0])
    def _():
        x = x_ref[...]
        a = jnp.dot(x, wg_ref[...], preferred_element_type=F32)
        u = jnp.dot(x, wu_ref[...], preferred_element_type=F32)
        act = _silu(a) * u * wt_ref[...]
        o_ref[...] = jnp.dot(act.astype(BF16), wd_ref[...], preferred_element_type=F32).astype(o_ref.dtype)

    @pl.when(i >= n_used_ref[0])
    def _():
        o_ref[...] = jnp.zeros_like(o_ref)


def moe_experts(xs, wts_sorted, tile_expert, n_used, lw):
    rows = xs.shape[0]
    n_tiles = rows // MOE_TILE
    grid_spec = pltpu.PrefetchScalarGridSpec(
        num_scalar_prefetch=2,
        grid=(n_tiles,),
        in_specs=[
            pl.BlockSpec((MOE_TILE, D_MODEL), lambda i, te, nu: (i, 0)),
            pl.BlockSpec((MOE_TILE, 1), lambda i, te, nu: (i, 0)),
            pl.BlockSpec((None, D_MODEL, D_FF), lambda i, te, nu: (te[i], 0, 0)),
            pl.BlockSpec((None, D_MODEL, D_FF), lambda i, te, nu: (te[i], 0, 0)),
            pl.BlockSpec((None, D_FF, D_MODEL), lambda i, te, nu: (te[i], 0, 0)),
        ],
        out_specs=pl.BlockSpec((MOE_TILE, D_MODEL), lambda i, te, nu: (i, 0)),
    )
    return pl.pallas_call(
        _moe_kernel,
        grid_spec=grid_spec,
        out_shape=jax.ShapeDtypeStruct((rows, D_MODEL), BF16),
        compiler_params=_cparams("arbitrary"),
        name="moe_experts",
    )(tile_expert, n_used, xs, wts_sorted, lw["w_gate_ff"], lw["w_up_ff"], lw["w_down_ff"])


def _moe_plan(ids, wts):
    two_t = ids.shape[0] * ids.shape[1]
    t = ids.shape[1]
    ef = ids.reshape(two_t)
    onehot = (ef[:, None] == jnp.arange(N_EXPERTS, dtype=I32)[None, :]).astype(I32)
    csum = jnp.cumsum(onehot, axis=0)
    rank = jnp.sum(onehot * csum, axis=1) - 1
    counts = csum[-1]
    padded = ((counts + MOE_TILE - 1) // MOE_TILE) * MOE_TILE
    ends = jnp.cumsum(padded)
    starts = ends - padded
    pos = jnp.sum(onehot * starts[None, :], axis=1) + rank
    n_rows = two_t + N_EXPERTS * MOE_TILE
    n_tiles = n_rows // MOE_TILE
    token = jnp.arange(two_t, dtype=I32) % t
    src = jnp.zeros((n_rows,), I32).at[pos].set(token)
    w_sorted = jnp.zeros((n_rows,), F32).at[pos].set(wts.reshape(two_t))
    tile_start = jnp.arange(n_tiles, dtype=I32) * MOE_TILE
    tile_expert = jnp.minimum(jnp.sum((tile_start[:, None] >= ends[None, :]).astype(I32), axis=1), N_EXPERTS - 1)
    n_used = (ends[-1] // MOE_TILE).reshape(1).astype(I32)
    return pos, src, w_sorted.reshape(n_rows, 1), tile_expert.astype(I32), n_used


def moe_ffn(h2, ids, wts, lw):
    pos, src, w_sorted, tile_expert, n_used = _moe_plan(ids, wts)
    xs = jnp.take(h2, src, axis=0)
    ys = moe_experts(xs, w_sorted, tile_expert, n_used, lw)
    return jnp.take(ys, pos, axis=0)


def _rope_tables(n_tokens):
    rows = n_tokens // GRID_W
    row = jnp.repeat(jnp.arange(rows, dtype=F32), GRID_W)
    col = jnp.tile(jnp.arange(GRID_W, dtype=F32), rows)
    n_freq = HEAD_DIM // 4
    inv = ROPE_THETA ** (-jnp.arange(n_freq, dtype=F32) / n_freq)
    ang = jnp.concatenate([row[:, None] * inv, col[:, None] * inv], axis=-1)
    cos = jnp.repeat(jnp.cos(ang), 2, axis=-1)
    sin = jnp.repeat(jnp.sin(ang), 2, axis=-1)
    sign = jnp.tile(jnp.array([-1.0, 1.0], F32), HEAD_DIM // 2)
    sin = sin * sign
    return cos, sin


def _prep_layer_weights(l, p):
    w_in = p["w_in"][l]
    gates = w_in[:, _O_GATE:]
    w_main = jnp.concatenate([gates, w_in[:, _O_Z:_O_XBC], w_in[:, _O_Q:_O_K], w_in[:, _O_PIN:_O_GATE],
                              w_in[:, _O_XBC:_O_DTF]], axis=1).astype(BF16)
    w_small = jnp.concatenate([w_in[:, _O_K:_O_PIN], w_in[:, _O_DTF:_O_Q],
                               jnp.zeros((D_MODEL, LANES - 2 * SSD_HEADS), F32)], axis=1).astype(BF16)
    pad_lanes = lambda v: jnp.concatenate([v.reshape(-1), jnp.zeros((LANES - v.size,), F32)]).reshape(1, LANES)
    return {
        "w_main": w_main,
        "w_small": w_small,
        "conv_w": jnp.concatenate([p["conv_w"][l], jnp.zeros((8 - CONV_W, XBC_DIM), F32)], axis=0),
        "conv_b": p["conv_b"][l].reshape(1, XBC_DIM),
        "dt_bias": pad_lanes(p["dt_bias"][l]),
        "a_log": pad_lanes(p["a_log"][l]),
        "d_skip": jnp.repeat(p["d_skip"][l], SSD_HEADDIM).reshape(1, SSD_INNER),
        "ssd_norm": p["ssd_norm"][l].reshape(1, SSD_INNER),
        "q_gain": jnp.tile(p["q_norm"][l], LANES // HEAD_DIM).reshape(1, LANES),
        "k_gain": jnp.tile(p["k_norm"][l], KV_HEADS).reshape(1, KV_INNER),
        "pool_w": p["pool_w"][l].astype(BF16),
        "pool_scale": p["pool_scale"][l].reshape(1, POOL_INNER),
        "w_ssd_out": p["w_ssd_out"][l].astype(BF16),
        "w_attn_out": p["w_attn_out"][l].astype(BF16),
        "w_pool_out": p["w_pool_out"][l].astype(BF16),
        "w_out": p["w_out"][l].astype(BF16),
        "norm_mix": p["norm_mix"][l].reshape(1, D_MODEL),
        "norm_ffn": p["norm_ffn"][l].reshape(1, D_MODEL),
        "w_gate_ff": p["w_gate_ff"][l].astype(BF16),
        "w_up_ff": p["w_up_ff"][l].astype(BF16),
        "w_down_ff": p["w_down_ff"][l].astype(BF16),
    }


def _mixer_stage(h, x, mods, lw, gw, path, n_seq, seq_len, tq, h0, rope, cache):
    t = n_seq * seq_len
    main = matmul(h, lw["w_main"], BF16, min(t, 1024), 1536, "in_proj_main")
    small = matmul(h, lw["w_small"], F32, min(t, 1024), SMALL_N, "in_proj_small")
    y_ssd, hf, hb = ssd_mixer(main, small, h0[0], h0[1], lw, n_seq, seq_len)
    if rope is None:
        k_norm, kt, v4 = kv_prep(small, lw["k_gain"], n_seq, seq_len)
        o_attn = attention(main, lw["q_gain"], kt, v4, n_seq, seq_len, tq)
    else:
        k_norm = None
        kt, v4 = kv_prep(small, lw["k_gain"], n_seq, seq_len, rope_kv=rope[0], cache=cache)
        o_attn = attention(main, lw["q_gain"], kt, v4, n_seq, seq_len, tq, rope_q=rope[1])
    o_pool = pool_mixer(main, lw["pool_w"], lw["pool_scale"], n_seq, seq_len)
    merged = merge_branches(y_ssd, o_attn, o_pool, main, lw)
    x_new, h2, ids, wts = outproj_route(x, merged, mods, lw, gw, path)
    return x_new, h2, ids, wts, k_norm, small, hf, hb


def kernel(x_prompt, x_sample, cache_k, cache_v, state_ssd_fwd, state_ssd_bwd, c, c_ctx, w_ada, b_ada, norm_mix,
           norm_ffn, w_in, conv_w, conv_b, a_log, dt_bias, d_skip, ssd_norm, q_norm, k_norm, pool_w, pool_scale,
           w_ssd_out, w_attn_out, w_pool_out, w_out, w_router, b_router, w_gate_ff, w_up_ff, w_down_ff, norm_final):
    p = dict(norm_mix=norm_mix, norm_ffn=norm_ffn, w_in=w_in, conv_w=conv_w, conv_b=conv_b, a_log=a_log,
             dt_bias=dt_bias, d_skip=d_skip, ssd_norm=ssd_norm, q_norm=q_norm, k_norm=k_norm, pool_w=pool_w,
             pool_scale=pool_scale, w_ssd_out=w_ssd_out, w_attn_out=w_attn_out, w_pool_out=w_pool_out, w_out=w_out,
             w_gate_ff=w_gate_ff, w_up_ff=w_up_ff, w_down_ff=w_down_ff)
    bp, lc, _ = x_prompt.shape
    depth = w_in.shape[0]
    bs, ls, _ = x_sample.shape
    past = cache_k.shape[2]
    tc, tl = bp * lc, bs * ls

    mod_rows = -(-(1 + bs) // 8) * 8
    cond = jnp.concatenate([c_ctx[None, :], c, jnp.zeros((mod_rows - 1 - bs, D_MODEL), F32)], axis=0)
    mods_all = ada_mods(cond, w_ada, b_ada).reshape(depth, mod_rows, 6, 1, D_MODEL)

    wr_hi, wr_lo = _split_bf16(w_router)
    zpad = jnp.zeros((D_MODEL, LANES - 2 * N_EXPERTS), BF16)
    gw = {
        "w_router_cat": jnp.concatenate([wr_hi, wr_lo, zpad], axis=1),
        "w_router_hi": jnp.concatenate([wr_hi, jnp.zeros((D_MODEL, LANES - N_EXPERTS), BF16)], axis=1),
        "b_router": b_router.reshape(N_EXPERTS, 1),
    }
    cos, sin = _rope_tables(ls)
    rope = ((jnp.tile(cos, (1, KV_HEADS)), jnp.tile(sin, (1, KV_HEADS))),
            (jnp.tile(cos, (1, LANES // HEAD_DIM)), jnp.tile(sin, (1, LANES // HEAD_DIM))))
    cache_k4 = cache_k.reshape(bs, depth, past, KV_INNER)
    cache_v4 = cache_v.reshape(bs, depth, past, KV_INNER)
    pairs = SSD_HEADS // 2
    st_f = state_ssd_fwd.reshape(bs, depth, pairs, 2 * SSD_HEADDIM, D_STATE)
    st_b = state_ssd_bwd.reshape(bs, depth, pairs, 2 * SSD_HEADDIM, D_STATE)

    path_c = (0, False, lc)
    path_l = (1, True, ls)
    xc = x_prompt.reshape(tc, D_MODEL)
    xl = x_sample.reshape(tl, D_MODEL)
    ks, vs, hfs, hbs = [], [], [], []
    hc = hl = None
    y2c = y2l = None
    lw = None
    for l in range(depth):
        lw_prev, lw = lw, _prep_layer_weights(l, p)
        mods = mods_all[l]
        if l == 0:
            hc = modulate_first(xc, mods, lw["norm_mix"], path_c)
            hl = modulate_first(xl, mods, lw["norm_mix"], path_l)
        else:
            xc, hc = residual_modulate(xc, y2c, mods_all[l - 1], mods, lw["norm_mix"], path_c)
            xl, hl = residual_modulate(xl, y2l, mods_all[l - 1], mods, lw["norm_mix"], path_l)
        xc, h2c, ids_c, wts_c, k_c, small_c, hf, hb = _mixer_stage(
            hc, xc, mods, lw, gw, path_c, bp, lc, lc, (None, None), None, None)
        ks.append(k_c.reshape(bp, lc, KV_HEADS, HEAD_DIM))
        vs.append(small_c[:, KV_INNER:2 * KV_INNER].reshape(bp, lc, KV_HEADS, HEAD_DIM))
        hfs.append(hf.reshape(bp, SSD_HEADS, SSD_HEADDIM, D_STATE))
        hbs.append(hb.reshape(bp, SSD_HEADS, SSD_HEADDIM, D_STATE))
        y2c = moe_ffn(h2c, ids_c, wts_c, lw)
        xl, h2l, ids_l, wts_l, _, _, _, _ = _mixer_stage(
            hl, xl, mods, lw, gw, path_l, bs, ls, CHUNK, (st_f[:, l], st_b[:, l]), rope, (cache_k4, cache_v4, l))
        y2l = moe_ffn(h2l, ids_l, wts_l, lw)
    gain_f = norm_final.reshape(1, D_MODEL)
    y_prompt = residual_final(xc, y2c, mods_all[depth - 1], gain_f, path_c).reshape(bp, lc, D_MODEL)
    y_sample = residual_final(xl, y2l, mods_all[depth - 1], gain_f, path_l).reshape(bs, ls, D_MODEL)
    return (y_prompt, y_sample, jnp.stack(ks, axis=1), jnp.stack(vs, axis=1), jnp.stack(hfs, axis=1),
            jnp.stack(hbs, axis=1))
```

```python
import functools

import jax
import jax.numpy as jnp
from jax import lax
from jax.experimental import pallas as pl
from jax.experimental.pallas import tpu as pltpu

F32 = jnp.float32
BF16 = jnp.bfloat16
I32 = jnp.int32

D_MODEL = 2048
GRID_W = 64
EPS = 1e-6

SSD_HEADS = 16
SSD_HEADDIM = 64
SSD_INNER = SSD_HEADS * SSD_HEADDIM
SSD_GROUPS = 4
D_STATE = 64
CONV_W = 5
CHUNK = 128
XBC_DIM = SSD_INNER + 2 * SSD_GROUPS * D_STATE

N_HEADS = 16
KV_HEADS = 4
HEAD_DIM = 64
ATTN_INNER = N_HEADS * HEAD_DIM
KV_INNER = KV_HEADS * HEAD_DIM
ROPE_THETA = 10000.0

POOL_GROUPS = 4
POOL_INNER = 1024
POOL_GROUP_DIM = POOL_INNER // POOL_GROUPS
POOL_WINDOWS = (2, 4, 8, 16)

N_EXPERTS = 16
N_EXPERT_GROUPS = 4
EXPERTS_PER_GROUP = N_EXPERTS // N_EXPERT_GROUPS
D_FF = 512

_O_Z = 0
_O_XBC = _O_Z + SSD_INNER
_O_DTF = _O_XBC + XBC_DIM
_O_DTB = _O_DTF + SSD_HEADS
_O_Q = _O_DTB + SSD_HEADS
_O_K = _O_Q + ATTN_INNER
_O_V = _O_K + KV_INNER
_O_PIN = _O_V + KV_INNER
_O_GATE = _O_PIN + POOL_INNER

MAIN_N = 3 * D_MODEL + SSD_INNER + ATTN_INNER + POOL_INNER + XBC_DIM
LANES = 128
SMALL_N = 2 * KV_INNER + LANES

HALO = 16
MOE_TILE = 256
VMEM_LIMIT = 56 * 1024 * 1024


def _cparams(*sem):
    return pltpu.CompilerParams(dimension_semantics=sem, vmem_limit_bytes=VMEM_LIMIT)


def _silu(x):
    return x * jax.nn.sigmoid(x)


def _rms(xf):
    return xf * lax.rsqrt(jnp.mean(xf * xf, axis=-1, keepdims=True) + EPS)


def _split_bf16(x):
    hi = x.astype(BF16)
    lo = (x - hi.astype(F32)).astype(BF16)
    return hi, lo


def _ada_kernel(cond_ref, w_ref, b_ref, o_ref):
    c = _silu(cond_ref[...])
    o_ref[...] = jnp.dot(c.astype(BF16), w_ref[...].astype(BF16), preferred_element_type=F32) + b_ref[...]


def ada_mods(cond, w_ada, b_ada, tn=1024):
    rows = cond.shape[0]
    depth, _, n = w_ada.shape
    return pl.pallas_call(
        _ada_kernel,
        grid=(depth, n // tn),
        in_specs=[
            pl.BlockSpec((rows, D_MODEL), lambda l, j: (0, 0)),
            pl.BlockSpec((None, D_MODEL, tn), lambda l, j: (l, 0, j)),
            pl.BlockSpec((None, 1, tn), lambda l, j: (l, 0, j)),
        ],
        out_specs=pl.BlockSpec((None, rows, tn), lambda l, j: (l, 0, j)),
        out_shape=jax.ShapeDtypeStruct((depth, rows, n), F32),
        compiler_params=_cparams("parallel", "parallel"),
        name="ada_mods",
    )(cond, w_ada, b_ada.reshape(depth, 1, n))


def _mod_spec(which, row_of_tile):
    return pl.BlockSpec((None, None, 1, D_MODEL), lambda i: (row_of_tile(i), which, 0, 0))


def _row_fn(path, tm):
    row0, per_seq, seq_len = path
    if per_seq:
        return lambda i: row0 + (i * tm) // seq_len
    return lambda i: row0


def _modulate_kernel(x_ref, g_ref, sh_ref, sc_ref, h_ref):
    xf = x_ref[...]
    h_ref[...] = (_rms(xf) * g_ref[...] * (1.0 + sc_ref[...]) + sh_ref[...]).astype(h_ref.dtype)


def modulate_first(x, mods, gain, path, tm=512):
    t = x.shape[0]
    rf = _row_fn(path, tm)
    return pl.pallas_call(
        _modulate_kernel,
        grid=(t // tm,),
        in_specs=[
            pl.BlockSpec((tm, D_MODEL), lambda i: (i, 0)),
            pl.BlockSpec((1, D_MODEL), lambda i: (0, 0)),
            _mod_spec(0, rf),
            _mod_spec(1, rf),
        ],
        out_specs=pl.BlockSpec((tm, D_MODEL), lambda i: (i, 0)),
        out_shape=jax.ShapeDtypeStruct((t, D_MODEL), BF16),
        compiler_params=_cparams("parallel"),
        name="modulate_first",
    )(x, gain, mods, mods)


def _moe_residual(x_ref, ya_ref, yb_ref, w_ref, g2_ref):
    w = w_ref[...]
    y = w[:, 0:1] * ya_ref[...].astype(F32) + w[:, 1:2] * yb_ref[...].astype(F32)
    return x_ref[...] + g2_ref[...] * y


def _residual_modulate_kernel(x_ref, ya_ref, yb_ref, w_ref, g2_ref, gain_ref, sh_ref, sc_ref, xo_ref, h_ref):
    xn = _moe_residual(x_ref, ya_ref, yb_ref, w_ref, g2_ref)
    xo_ref[...] = xn
    h_ref[...] = (_rms(xn) * gain_ref[...] * (1.0 + sc_ref[...]) + sh_ref[...]).astype(h_ref.dtype)


def residual_modulate(x, y2, wcol, mods_prev, mods_next, gain_next, path, tm=512):
    t = x.shape[0]
    nt = t // tm
    rf = _row_fn(path, tm)
    return pl.pallas_call(
        _residual_modulate_kernel,
        grid=(nt,),
        in_specs=[
            pl.BlockSpec((tm, D_MODEL), lambda i: (i, 0)),
            pl.BlockSpec((tm, D_MODEL), lambda i: (i, 0)),
            pl.BlockSpec((tm, D_MODEL), lambda i: (i + nt, 0)),
            pl.BlockSpec((tm, LANES), lambda i: (i, 0)),
            _mod_spec(5, rf),
            pl.BlockSpec((1, D_MODEL), lambda i: (0, 0)),
            _mod_spec(0, rf),
            _mod_spec(1, rf),
        ],
        out_specs=[
            pl.BlockSpec((tm, D_MODEL), lambda i: (i, 0)),
            pl.BlockSpec((tm, D_MODEL), lambda i: (i, 0)),
        ],
        out_shape=[
            jax.ShapeDtypeStruct((t, D_MODEL), F32),
            jax.ShapeDtypeStruct((t, D_MODEL), BF16),
        ],
        compiler_params=_cparams("parallel"),
        name="residual_modulate",
    )(x, y2, y2, wcol, mods_prev, gain_next, mods_next, mods_next)


def _residual_final_kernel(x_ref, ya_ref, yb_ref, w_ref, g2_ref, gain_ref, y_ref):
    xn = _moe_residual(x_ref, ya_ref, yb_ref, w_ref, g2_ref)
    y_ref[...] = _rms(xn) * gain_ref[...]


def residual_final(x, y2, wcol, mods_prev, gain, path, tm=512):
    t = x.shape[0]
    nt = t // tm
    rf = _row_fn(path, tm)
    return pl.pallas_call(
        _residual_final_kernel,
        grid=(nt,),
        in_specs=[
            pl.BlockSpec((tm, D_MODEL), lambda i: (i, 0)),
            pl.BlockSpec((tm, D_MODEL), lambda i: (i, 0)),
            pl.BlockSpec((tm, D_MODEL), lambda i: (i + nt, 0)),
            pl.BlockSpec((tm, LANES), lambda i: (i, 0)),
            _mod_spec(5, rf),
            pl.BlockSpec((1, D_MODEL), lambda i: (0, 0)),
        ],
        out_specs=pl.BlockSpec((tm, D_MODEL), lambda i: (i, 0)),
        out_shape=jax.ShapeDtypeStruct((t, D_MODEL), F32),
        compiler_params=_cparams("parallel"),
        name="residual_final",
    )(x, y2, y2, wcol, mods_prev, gain)


def _mm_kernel(a_ref, b_ref, o_ref):
    o_ref[...] = jnp.dot(a_ref[...], b_ref[...], preferred_element_type=F32).astype(o_ref.dtype)


def matmul(a, b, out_dtype, tm, tn, name):
    m, k = a.shape
    n = b.shape[1]
    return pl.pallas_call(
        _mm_kernel,
        grid=(m // tm, n // tn),
        in_specs=[
            pl.BlockSpec((tm, k), lambda i, j: (i, 0)),
            pl.BlockSpec((k, tn), lambda i, j: (0, j)),
        ],
        out_specs=pl.BlockSpec((tm, tn), lambda i, j: (i, j)),
        out_shape=jax.ShapeDtypeStruct((m, n), out_dtype),
        compiler_params=_cparams("parallel", "arbitrary"),
        name=name,
    )(a, b)


def _chunk_window(ref, c, n_chunks, seq_len):
    s = pl.multiple_of(c * CHUNK, CHUNK)
    cur = ref[pl.ds(s, CHUNK), :].astype(F32)
    sp = pl.multiple_of(jnp.maximum(s - HALO, 0), HALO)
    sn = pl.multiple_of(jnp.minimum(s + CHUNK, seq_len - HALO), HALO)
    prev = ref[pl.ds(sp, HALO), :].astype(F32)
    nxt = ref[pl.ds(sn, HALO), :].astype(F32)
    prev = jnp.where(c > 0, prev, 0.0)
    nxt = jnp.where(c < n_chunks - 1, nxt, 0.0)
    return jnp.concatenate([prev, cur, nxt], axis=0)


def _ssd_kernel(*refs, seq_len, has_h0):
    if has_h0:
        (xbc_ref, z_ref, dt_ref, h0f_ref, h0b_ref, cw_ref, cb_ref, dtb_ref, alog_ref, dskip_ref, gain_ref,
         expand_ref, colsel_ref, y_ref, hf_ref, hb_ref, xc_ref, sp_ref, yacc_ref, st_ref) = refs
    else:
        (xbc_ref, z_ref, dt_ref, cw_ref, cb_ref, dtb_ref, alog_ref, dskip_ref, gain_ref,
         expand_ref, colsel_ref, y_ref, hf_ref, hb_ref, xc_ref, sp_ref, yacc_ref, st_ref) = refs
        h0f_ref = h0b_ref = None
    n_chunks = seq_len // CHUNK
    win = CHUNK + 2 * HALO

    def conv_chunk(c, carry):
        w = _chunk_window(xbc_ref, c, n_chunks, seq_len)
        acc = jnp.zeros((CHUNK, XBC_DIM), F32) + cb_ref[...]
        for j in range(CONV_W):
            shift = (CONV_W // 2 - j) % win
            wj = w if shift == 0 else pltpu.roll(w, shift, 0)
            acc = acc + cw_ref[j:j + 1, :] * wj[HALO:HALO + CHUNK, :]
        s = pl.multiple_of(c * CHUNK, CHUNK)
        xc_ref[pl.ds(s, CHUNK), :] = _silu(acc).astype(BF16)
        return carry

    lax.fori_loop(0, n_chunks, conv_chunk, 0)

    xdt = dt_ref[...] + dtb_ref[...]
    sp_ref[...] = jnp.maximum(xdt, 0.0) + jnp.log1p(jnp.exp(-jnp.abs(xdt)))
    a_row = -jnp.exp(alog_ref[...])

    rows = lax.broadcasted_iota(I32, (CHUNK, CHUNK), 0)
    cols = lax.broadcasted_iota(I32, (CHUNK, CHUNK), 1)
    lane_lo = cols < SSD_HEADDIM

    def cat2(x):
        hi, lo = _split_bf16(x)
        return jnp.concatenate([hi, lo], axis=1)

    def scan_chunk(c, forward):
        d = 0 if forward else 1
        off = d * SSD_HEADS
        s = pl.multiple_of(c * CHUNK, CHUNK)
        xcv = xc_ref[pl.ds(s, CHUNK), :]
        dtc = sp_ref[pl.ds(s, CHUNK), :]
        a = dtc * a_row
        tri = (rows >= cols) if forward else (cols >= rows)
        tri_b = jnp.where(tri, 1.0, 0.0).astype(BF16)
        a1 = a.astype(BF16)
        r1 = a - a1.astype(F32)
        a2 = r1.astype(BF16)
        a3 = (r1 - a2.astype(F32)).astype(BF16)
        cs = (jnp.dot(tri_b, a1, preferred_element_type=F32) + jnp.dot(tri_b, a2, preferred_element_type=F32)
              + jnp.dot(tri_b, a3, preferred_element_type=F32))
        cs_t = cs.T
        tot = cs[CHUNK - 1:CHUNK, :] if forward else cs[0:1, :]
        expand = expand_ref[d]
        dt_full = jnp.dot(cat2(dtc), expand, preferred_element_type=F32)
        dout_full = jnp.dot(cat2(jnp.exp(cs)), expand, preferred_element_type=F32)
        dst_full = jnp.dot(cat2(jnp.exp(tot - cs)), expand, preferred_element_type=F32)
        etot_full = dout_full[CHUNK - 1:CHUNK, :] if forward else dout_full[0:1, :]
        colb = jnp.dot(cat2(cs), colsel_ref[d], preferred_element_type=F32)
        xd = xcv[:, :SSD_INNER].astype(F32) * dt_full
        xdb = xd.astype(BF16)
        xdd = (xd * dst_full).astype(BF16)
        b_t = xcv[:, SSD_INNER:SSD_INNER + SSD_GROUPS * D_STATE].astype(F32).T.astype(BF16)
        zero = jnp.zeros((CHUNK, CHUNK), BF16)
        ys = []
        for g in range(SSD_GROUPS):
            bg_t = b_t[g * D_STATE:(g + 1) * D_STATE, :]
            cg = xcv[:, SSD_INNER + (SSD_GROUPS + g) * D_STATE:SSD_INNER + (SSD_GROUPS + g + 1) * D_STATE]
            gmat = jnp.dot(cg, bg_t, preferred_element_type=F32)
            for pp in range(2):
                j = 2 * g + pp
                ms = []
                for hh in range(2):
                    h = 2 * j + hh
                    diff = colb[:, h * CHUNK:(h + 1) * CHUNK] - cs_t[off + h:off + h + 1, :]
                    lm = jnp.exp(jnp.where(tri, diff, -jnp.inf))
                    ms.append((gmat * lm).astype(BF16))
                mcat = jnp.concatenate(ms, axis=1)
                xp = xdb[:, j * CHUNK:(j + 1) * CHUNK]
                xblk = jnp.concatenate([jnp.where(lane_lo, xp, zero), jnp.where(lane_lo, zero, xp)], axis=0)
                y_diag = jnp.dot(mcat, xblk, preferred_element_type=F32)
                st = st_ref[j]
                y_off = jnp.dot(cg, st.astype(BF16), preferred_element_type=F32) * dout_full[:, j * CHUNK:(j + 1) * CHUNK]
                contrib = jnp.dot(bg_t, xdd[:, j * CHUNK:(j + 1) * CHUNK], preferred_element_type=F32)
                st_ref[j] = st * etot_full[:, j * CHUNK:(j + 1) * CHUNK] + contrib
                ys.append(y_diag + y_off)
        return s, xcv, jnp.concatenate(ys, axis=1)

    def fwd_chunk(c, carry):
        s, _, y = scan_chunk(c, True)
        yacc_ref[pl.ds(s, CHUNK), :] = y
        return carry

    def bwd_chunk(i, carry):
        c = n_chunks - 1 - i
        s, xcv, y = scan_chunk(c, False)
        xs = xcv[:, :SSD_INNER].astype(F32)
        y = yacc_ref[pl.ds(s, CHUNK), :] + y + dskip_ref[...] * xs
        y = y * _silu(z_ref[pl.ds(s, CHUNK), :].astype(F32))
        y_ref[pl.ds(s, CHUNK), :] = (_rms(y) * gain_ref[...]).astype(y_ref.dtype)
        return carry

    if has_h0:
        st_ref[...] = h0f_ref[...]
    else:
        st_ref[...] = jnp.zeros_like(st_ref)
    lax.fori_loop(0, n_chunks, fwd_chunk, 0)
    hf_ref[...] = st_ref[...]
    if has_h0:
        st_ref[...] = h0b_ref[...]
    else:
        st_ref[...] = jnp.zeros_like(st_ref)
    lax.fori_loop(0, n_chunks, bwd_chunk, 0)
    hb_ref[...] = st_ref[...]


def ssd_mixer(main, small, h0f, h0b, lw, n_seq, seq_len):
    t = n_seq * seq_len
    has_h0 = h0f is not None
    pairs = SSD_HEADS // 2
    st_shape = (pairs, D_STATE, 2 * SSD_HEADDIM)
    st_spec = pl.BlockSpec((None,) + st_shape, lambda b: (b, 0, 0, 0))
    full = lambda shape: pl.BlockSpec(shape, lambda b: (0,) * len(shape))
    in_specs = [
        pl.BlockSpec((seq_len, XBC_DIM), lambda b: (b, (MAIN_N - XBC_DIM) // XBC_DIM)),
        pl.BlockSpec((seq_len, SSD_INNER), lambda b: (b, 3 * D_MODEL // SSD_INNER)),
        pl.BlockSpec((seq_len, LANES), lambda b: (b, 2 * KV_INNER // LANES)),
    ]
    args = [main, main, small]
    if has_h0:
        in_specs += [st_spec, st_spec]
        args += [h0f, h0b]
    k_head = (jnp.arange(2 * LANES, dtype=I32) % LANES)[None, :, None] - SSD_HEADS * jnp.arange(2, dtype=I32)[:, None, None]
    expand = (k_head == (jnp.arange(SSD_INNER, dtype=I32) // SSD_HEADDIM)[None, None, :]).astype(BF16)
    colsel = (k_head == (jnp.arange(SSD_HEADS * CHUNK, dtype=I32) // CHUNK)[None, None, :]).astype(BF16)
    in_specs += [full((8, XBC_DIM)), full((1, XBC_DIM)), full((1, LANES)), full((1, LANES)),
                 full((1, SSD_INNER)), full((1, SSD_INNER)), full((2, 2 * LANES, SSD_INNER)),
                 full((2, 2 * LANES, SSD_HEADS * CHUNK))]
    args += [lw["conv_w"], lw["conv_b"], lw["dt_bias"], lw["a_log"], lw["d_skip"], lw["ssd_norm"], expand, colsel]
    return pl.pallas_call(
        functools.partial(_ssd_kernel, seq_len=seq_len, has_h0=has_h0),
        grid=(n_seq,),
        in_specs=in_specs,
        out_specs=[pl.BlockSpec((seq_len, SSD_INNER), lambda b: (b, 0)), st_spec, st_spec],
        out_shape=[
            jax.ShapeDtypeStruct((t, SSD_INNER), BF16),
            jax.ShapeDtypeStruct((n_seq,) + st_shape, F32),
            jax.ShapeDtypeStruct((n_seq,) + st_shape, F32),
        ],
        scratch_shapes=[
            pltpu.VMEM((seq_len, XBC_DIM), BF16),
            pltpu.VMEM((seq_len, LANES), F32),
            pltpu.VMEM((seq_len, SSD_INNER), F32),
            pltpu.VMEM(st_shape, F32),
        ],
        compiler_params=_cparams("parallel"),
        name="ssd_mixer",
    )(*args)


def _swap_pairs(x):
    n = x.shape[-1]
    lane = lax.broadcasted_iota(I32, x.shape, x.ndim - 1)
    return jnp.where(lane % 2 == 0, pltpu.roll(x, n - 1, x.ndim - 1), pltpu.roll(x, 1, x.ndim - 1))


def _kv_prep_kernel(*refs, seq_len, rope):
    if rope:
        (k_ref, v_ref, gain_ref, cos_ref, sin_ref, ck_ref, cv_ref, kt_ref, v4_ref) = refs
    else:
        (k_ref, v_ref, gain_ref, kn_ref, kt_ref, v4_ref) = refs
    kf = k_ref[...]
    parts = []
    for g in range(KV_HEADS):
        kh = kf[:, g * HEAD_DIM:(g + 1) * HEAD_DIM]
        parts.append(kh * lax.rsqrt(jnp.mean(kh * kh, axis=-1, keepdims=True) + EPS))
    kn = jnp.concatenate(parts, axis=1) * gain_ref[...]
    if rope:
        kn = kn * cos_ref[...] + _swap_pairs(kn) * sin_ref[...]
    else:
        kn_ref[...] = kn
    kt_ref[:, 0:seq_len] = kn.T.astype(BF16)
    def with_ones(vh):
        return jnp.concatenate([vh, jnp.ones_like(vh)], axis=1).astype(BF16)

    vf = v_ref[...]
    for g in range(KV_HEADS):
        v4_ref[g, 0:seq_len, :] = with_ones(vf[:, g * HEAD_DIM:(g + 1) * HEAD_DIM])
    if rope:
        past = ck_ref.shape[0]
        kt_ref[:, seq_len:seq_len + past] = ck_ref[...].T.astype(BF16)
        cvf = cv_ref[...]
        for g in range(KV_HEADS):
            v4_ref[g, seq_len:seq_len + past, :] = with_ones(cvf[:, g * HEAD_DIM:(g + 1) * HEAD_DIM])


def kv_prep(small, k_gain, n_seq, seq_len, rope_kv=None, cache=None):
    t = n_seq * seq_len
    rope = rope_kv is not None
    in_specs = [
        pl.BlockSpec((seq_len, KV_INNER), lambda b: (b, 0)),
        pl.BlockSpec((seq_len, KV_INNER), lambda b: (b, 1)),
        pl.BlockSpec((1, KV_INNER), lambda b: (0, 0)),
    ]
    args = [small, small, k_gain]
    n_keys = seq_len
    out_specs, out_shape = [], []
    if rope:
        cache_k, cache_v, layer = cache
        past = cache_k.shape[2]
        n_keys += past
        in_specs += [
            pl.BlockSpec((seq_len, KV_INNER), lambda b: (0, 0)),
            pl.BlockSpec((seq_len, KV_INNER), lambda b: (0, 0)),
            pl.BlockSpec((None, None, past, KV_INNER), lambda b: (b, layer, 0, 0)),
            pl.BlockSpec((None, None, past, KV_INNER), lambda b: (b, layer, 0, 0)),
        ]
        args += [rope_kv[0], rope_kv[1], cache_k, cache_v]
    else:
        out_specs.append(pl.BlockSpec((seq_len, KV_INNER), lambda b: (b, 0)))
        out_shape.append(jax.ShapeDtypeStruct((t, KV_INNER), F32))
    out_specs += [
        pl.BlockSpec((None, KV_INNER, n_keys), lambda b: (b, 0, 0)),
        pl.BlockSpec((None, KV_HEADS, n_keys, 2 * HEAD_DIM), lambda b: (b, 0, 0, 0)),
    ]
    out_shape += [
        jax.ShapeDtypeStruct((n_seq, KV_INNER, n_keys), BF16),
        jax.ShapeDtypeStruct((n_seq, KV_HEADS, n_keys, 2 * HEAD_DIM), BF16),
    ]
    return pl.pallas_call(
        functools.partial(_kv_prep_kernel, seq_len=seq_len, rope=rope),
        grid=(n_seq,),
        in_specs=in_specs,
        out_specs=out_specs,
        out_shape=out_shape,
        compiler_params=_cparams("parallel"),
        name="kv_prep",
    )(*args)


def _attn_kernel(*refs, rope, tq):
    if rope:
        q_ref, gain_ref, cos_ref, sin_ref, kt_ref, v4_ref, o_ref = refs
    else:
        q_ref, gain_ref, kt_ref, v4_ref, o_ref = refs
    rep = N_HEADS // KV_HEADS
    scale = HEAD_DIM ** -0.5
    heads = []
    for j in range(N_HEADS // 2):
        qs = q_ref[:, j * LANES:(j + 1) * LANES].astype(F32)
        qg = qs * gain_ref[...]
        if rope:
            qg = qg * cos_ref[...] + _swap_pairs(qg) * sin_ref[...]
        for hh in range(2):
            raw = qs[:, hh * HEAD_DIM:(hh + 1) * HEAD_DIM]
            inv = lax.rsqrt(jnp.mean(raw * raw, axis=-1, keepdims=True) + EPS) * scale
            heads.append((qg[:, hh * HEAD_DIM:(hh + 1) * HEAD_DIM] * inv).astype(BF16))
    for g in range(KV_HEADS):
        qstack = jnp.concatenate(heads[g * rep:(g + 1) * rep], axis=0)
        kt = kt_ref[g * HEAD_DIM:(g + 1) * HEAD_DIM, :]
        s = jnp.dot(qstack, kt, preferred_element_type=F32)
        m = jnp.max(s, axis=-1, keepdims=True)
        p = jnp.exp((s - m).astype(BF16))
        ov = jnp.dot(p, v4_ref[g], preferred_element_type=F32)
        o = ov[:, :HEAD_DIM] / ov[:, HEAD_DIM:HEAD_DIM + 1]
        for r in range(rep):
            h = g * rep + r
            o_ref[:, h * HEAD_DIM:(h + 1) * HEAD_DIM] = o[r * tq:(r + 1) * tq, :].astype(o_ref.dtype)


def attention(main, q_gain, kt, v4, n_seq, seq_len, tq, rope_q=None):
    t = n_seq * seq_len
    nq = seq_len // tq
    n_keys = kt.shape[2]
    rope = rope_q is not None
    in_specs = [
        pl.BlockSpec((tq, ATTN_INNER), lambda b, i: (b * nq + i, (3 * D_MODEL + SSD_INNER) // ATTN_INNER)),
        pl.BlockSpec((1, LANES), lambda b, i: (0, 0)),
    ]
    args = [main, q_gain]
    if rope:
        in_specs += [pl.BlockSpec((tq, LANES), lambda b, i: (i, 0)), pl.BlockSpec((tq, LANES), lambda b, i: (i, 0))]
        args += [rope_q[0], rope_q[1]]
    in_specs += [
        pl.BlockSpec((None, KV_INNER, n_keys), lambda b, i: (b, 0, 0)),
        pl.BlockSpec((None, KV_HEADS, n_keys, 2 * HEAD_DIM), lambda b, i: (b, 0, 0, 0)),
    ]
    args += [kt, v4]
    return pl.pallas_call(
        functools.partial(_attn_kernel, rope=rope, tq=tq),
        grid=(n_seq, nq),
        in_specs=in_specs,
        out_specs=pl.BlockSpec((tq, ATTN_INNER), lambda b, i: (b * nq + i, 0)),
        out_shape=jax.ShapeDtypeStruct((t, ATTN_INNER), BF16),
        compiler_params=_cparams("parallel", "arbitrary"),
        name="attention",
    )(*args)


def _pool_kernel(p_ref, w_ref, scale_ref, o_ref, *, seq_len):
    n_chunks = seq_len // CHUNK
    win = CHUNK + 2 * HALO
    t_loc = lax.broadcasted_iota(I32, (CHUNK, win), 0)
    r_loc = lax.broadcasted_iota(I32, (CHUNK, win), 1) - HALO
    t_col = lax.broadcasted_iota(I32, (CHUNK, 1), 0)

    def chunk(c, carry):
        s = pl.multiple_of(c * CHUNK, CHUNK)
        w = _chunk_window(p_ref, c, n_chunks, seq_len).astype(BF16)
        outs = []
        for g, width in enumerate(POOL_WINDOWS):
            half = width // 2
            d = r_loc - t_loc
            band = jnp.where((d >= -half) & (d < width - half), 1.0, 0.0).astype(BF16)
            wg = w[:, g * POOL_GROUP_DIM:(g + 1) * POOL_GROUP_DIM]
            sums = jnp.dot(band, wg, preferred_element_type=F32)
            tg = t_col + s
            cnt = (jnp.minimum(tg + (width - half), seq_len) - jnp.maximum(tg - half, 0)).astype(F32)
            cur = wg[HALO:HALO + CHUNK, :].astype(F32)
            diff = sums / cnt - cur
            outs.append(jnp.dot(diff.astype(BF16), w_ref[g], preferred_element_type=F32))
        o_ref[pl.ds(s, CHUNK), :] = (jnp.concatenate(outs, axis=1) * scale_ref[...]).astype(o_ref.dtype)
        return carry

    lax.fori_loop(0, n_chunks, chunk, 0)


def pool_mixer(main, pool_w, pool_scale, n_seq, seq_len):
    t = n_seq * seq_len
    return pl.pallas_call(
        functools.partial(_pool_kernel, seq_len=seq_len),
        grid=(n_seq,),
        in_specs=[
            pl.BlockSpec((seq_len, POOL_INNER), lambda b: (b, (3 * D_MODEL + SSD_INNER + ATTN_INNER) // POOL_INNER)),
            pl.BlockSpec((POOL_GROUPS, POOL_GROUP_DIM, POOL_GROUP_DIM), lambda b: (0, 0, 0)),
            pl.BlockSpec((1, POOL_INNER), lambda b: (0, 0)),
        ],
        out_specs=pl.BlockSpec((seq_len, POOL_INNER), lambda b: (b, 0)),
        out_shape=jax.ShapeDtypeStruct((t, POOL_INNER), BF16),
        compiler_params=_cparams("parallel"),
        name="pool_mixer",
    )(main, pool_w, pool_scale)


def _merge_kernel(ys_ref, ya_ref, yp_ref, g0_ref, g1_ref, g2_ref, ws_ref, wa_ref, wp_ref, o_ref):
    acc = jax.nn.sigmoid(g0_ref[...].astype(F32)) * jnp.dot(ys_ref[...], ws_ref[...], preferred_element_type=F32)
    acc = acc + jax.nn.sigmoid(g1_ref[...].astype(F32)) * jnp.dot(ya_ref[...], wa_ref[...],
                                                                  preferred_element_type=F32)
    acc = acc + jax.nn.sigmoid(g2_ref[...].astype(F32)) * jnp.dot(yp_ref[...], wp_ref[...],
                                                                  preferred_element_type=F32)
    o_ref[...] = acc.astype(o_ref.dtype)


def merge_branches(y_ssd, o_attn, o_pool, main, lw, tm=512):
    t = y_ssd.shape[0]
    act = lambda: pl.BlockSpec((tm, SSD_INNER), lambda i: (i, 0))
    gate = lambda k: pl.BlockSpec((tm, D_MODEL), lambda i: (i, k))
    wspec = lambda: pl.BlockSpec((SSD_INNER, D_MODEL), lambda i: (0, 0))
    return pl.pallas_call(
        _merge_kernel,
        grid=(t // tm,),
        in_specs=[act(), act(), act(), gate(0), gate(1), gate(2), wspec(), wspec(), wspec()],
        out_specs=pl.BlockSpec((tm, D_MODEL), lambda i: (i, 0)),
        out_shape=jax.ShapeDtypeStruct((t, D_MODEL), BF16),
        compiler_params=_cparams("parallel"),
        name="merge_branches",
    )(y_ssd, o_attn, o_pool, main, main, main, lw["w_ssd_out"], lw["w_attn_out"], lw["w_pool_out"])


def _route(logits_t, bias_col):
    scores = jax.nn.sigmoid(logits_t)
    biased = scores + bias_col
    sc = [scores[e:e + 1, :] for e in range(N_EXPERTS)]
    bi = [biased[e:e + 1, :] for e in range(N_EXPERTS)]
    group_scores = []
    for g in range(N_EXPERT_GROUPS):
        a, b, c, d = bi[4 * g:4 * g + 4]
        hi1, lo1 = jnp.maximum(a, b), jnp.minimum(a, b)
        hi2, lo2 = jnp.maximum(c, d), jnp.minimum(c, d)
        top1 = jnp.maximum(hi1, hi2)
        top2 = jnp.maximum(jnp.minimum(hi1, hi2), jnp.maximum(lo1, lo2))
        group_scores.append(top1 + top2)
    best = jnp.zeros_like(group_scores[0], dtype=I32)
    best_v = group_scores[0]
    for g in range(1, N_EXPERT_GROUPS):
        upd = group_scores[g] > best_v
        best = jnp.where(upd, g, best)
        best_v = jnp.where(upd, group_scores[g], best_v)
    vb, vs = [], []
    for j in range(EXPERTS_PER_GROUP):
        b_j, s_j = bi[j], sc[j]
        for g in range(1, N_EXPERT_GROUPS):
            sel = best == g
            b_j = jnp.where(sel, bi[4 * g + j], b_j)
            s_j = jnp.where(sel, sc[4 * g + j], s_j)
        vb.append(b_j)
        vs.append(s_j)

    def first_argmax(vals, excluded=None):
        idx = None
        for j in range(EXPERTS_PER_GROUP):
            v = vals[j] if excluded is None else jnp.where(excluded == j, -jnp.inf, vals[j])
            if idx is None:
                idx, cur = jnp.zeros_like(best), v
            else:
                upd = v > cur
                idx = jnp.where(upd, j, idx)
                cur = jnp.where(upd, v, cur)
        return idx

    i0 = first_argmax(vb)
    i1 = first_argmax(vb, excluded=i0)

    def pick(vals, idx):
        out = vals[0]
        for j in range(1, EXPERTS_PER_GROUP):
            out = jnp.where(idx == j, vals[j], out)
        return out

    s0, s1 = pick(vs, i0), pick(vs, i1)
    tot = s0 + s1
    ids = jnp.concatenate([best * EXPERTS_PER_GROUP + i0, best * EXPERTS_PER_GROUP + i1], axis=0)
    wts = jnp.concatenate([s0 / tot, s1 / tot], axis=0)
    return ids, wts


def _outproj_kernel(x_ref, m_ref, wo_ref, g1_ref, gain_ref, sh_ref, sc_ref, wr_ref, wr_hi_ref, br_ref,
                    xo_ref, h_ref, ids_ref, rank_ref, wcol_ref, cnt_ref, run_ref):
    tm = x_ref.shape[0]

    @pl.when(pl.program_id(0) == 0)
    def _():
        run_ref[...] = jnp.zeros_like(run_ref)

    mix = jnp.dot(m_ref[...], wo_ref[...], preferred_element_type=F32)
    xn = x_ref[...] + g1_ref[...] * mix
    xo_ref[...] = xn
    h = _rms(xn) * gain_ref[...] * (1.0 + sc_ref[...]) + sh_ref[...]
    h_hi, h_lo = _split_bf16(h)
    h_ref[...] = h
    lg = jnp.dot(h_hi, wr_ref[...], preferred_element_type=F32) + jnp.dot(h_lo, wr_hi_ref[...],
                                                                         preferred_element_type=F32)
    lg_t = lg.T
    logits_t = lg_t[0:N_EXPERTS, :] + lg_t[N_EXPERTS:2 * N_EXPERTS, :]
    ids, wts = _route(logits_t, br_ref[...])
    ids_ref[...] = ids
    e_iota = lax.broadcasted_iota(I32, (N_EXPERTS, tm), 0)
    earlier = (lax.broadcasted_iota(I32, (tm, tm), 0) < lax.broadcasted_iota(I32, (tm, tm), 1))
    earlier = jnp.where(earlier, 1.0, 0.0).astype(BF16)
    oh0 = jnp.where(e_iota == ids[0:1, :], 1.0, 0.0)
    oh1 = jnp.where(e_iota == ids[1:2, :], 1.0, 0.0)
    before0 = jnp.dot(oh0.astype(BF16), earlier, preferred_element_type=F32)
    before1 = jnp.dot(oh1.astype(BF16), earlier, preferred_element_type=F32)
    c0 = jnp.sum(oh0, axis=1, keepdims=True)
    c1 = jnp.sum(oh1, axis=1, keepdims=True)
    running = run_ref[...]
    rank0 = jnp.sum(oh0 * (before0 + running), axis=0, keepdims=True)
    rank1 = jnp.sum(oh1 * (before1 + running + c0), axis=0, keepdims=True)
    rank_ref[...] = jnp.concatenate([rank0, rank1], axis=0).astype(I32)
    running = running + c0 + c1
    run_ref[...] = running
    cnt_ref[...] = jnp.broadcast_to(running, cnt_ref.shape)
    r_iota = lax.broadcasted_iota(I32, (LANES, tm), 0)
    w_rows = jnp.where(r_iota == 0, wts[0:1, :], jnp.where(r_iota == 1, wts[1:2, :], 0.0))
    wcol_ref[...] = w_rows.T


def outproj_route(x, merged, mods, lw, gw, path, tm=512):
    t = x.shape[0]
    rf = _row_fn(path, tm)
    row = lambda: pl.BlockSpec((1, D_MODEL), lambda i: (0, 0))
    return pl.pallas_call(
        _outproj_kernel,
        grid=(t // tm,),
        in_specs=[
            pl.BlockSpec((tm, D_MODEL), lambda i: (i, 0)),
            pl.BlockSpec((tm, D_MODEL), lambda i: (i, 0)),
            pl.BlockSpec((D_MODEL, D_MODEL), lambda i: (0, 0)),
            _mod_spec(2, rf),
            row(),
            _mod_spec(3, rf),
            _mod_spec(4, rf),
            pl.BlockSpec((D_MODEL, LANES), lambda i: (0, 0)),
            pl.BlockSpec((D_MODEL, LANES), lambda i: (0, 0)),
            pl.BlockSpec((N_EXPERTS, 1), lambda i: (0, 0)),
        ],
        out_specs=[
            pl.BlockSpec((tm, D_MODEL), lambda i: (i, 0)),
            pl.BlockSpec((tm, D_MODEL), lambda i: (i, 0)),
            pl.BlockSpec((2, tm), lambda i: (0, i)),
            pl.BlockSpec((2, tm), lambda i: (0, i)),
            pl.BlockSpec((tm, LANES), lambda i: (i, 0)),
            pl.BlockSpec((N_EXPERTS, LANES), lambda i: (0, 0)),
        ],
        out_shape=[
            jax.ShapeDtypeStruct((t, D_MODEL), F32),
            jax.ShapeDtypeStruct((t, D_MODEL), F32),
            jax.ShapeDtypeStruct((2, t), I32),
            jax.ShapeDtypeStruct((2, t), I32),
            jax.ShapeDtypeStruct((t, LANES), F32),
            jax.ShapeDtypeStruct((N_EXPERTS, LANES), F32),
        ],
        scratch_shapes=[pltpu.VMEM((N_EXPERTS, 1), F32)],
        compiler_params=_cparams("arbitrary"),
        name="outproj_route",
    )(x, merged, lw["w_out"], mods, lw["norm_ffn"], mods, mods, gw["w_router_cat"], gw["w_router_hi"],
      gw["b_router"])


def _moe_kernel(tile_expert_ref, src_ref, h_hbm, wg_ref, wu_ref, wd_ref, o_ref, xbuf0, xbuf1, sem, *, n_tiles):
    i = pl.program_id(0)
    bufs = (xbuf0, xbuf1)

    def start_tile(tile, slot):
        for r in range(MOE_TILE):
            row = src_ref[tile * MOE_TILE + r]
            pltpu.make_async_copy(h_hbm.at[pl.ds(row, 1)], bufs[slot].at[pl.ds(r, 1)], sem.at[slot]).start()

    def wait_tile(slot):
        pltpu.make_async_copy(h_hbm.at[pl.ds(0, MOE_TILE)], bufs[slot], sem.at[slot]).wait()

    def compute(slot):
        x = bufs[slot][...].astype(BF16)
        a = jnp.dot(x, wg_ref[...], preferred_element_type=F32)
        u = jnp.dot(x, wu_ref[...], preferred_element_type=F32)
        act = _silu(a) * u
        o_ref[...] = jnp.dot(act.astype(BF16), wd_ref[...], preferred_element_type=F32).astype(o_ref.dtype)

    @pl.when(i == 0)
    def _():
        start_tile(0, 0)

    for slot in range(2):
        @pl.when((i % 2 == slot) & (i < n_tiles - 1))
        def _(slot=slot):
            wait_tile(slot)
            start_tile(i + 1, 1 - slot)
            compute(slot)

    @pl.when(i == n_tiles - 1)
    def _():
        wait_tile((n_tiles - 1) % 2)
        compute((n_tiles - 1) % 2)


def moe_experts(h, src, tile_expert, lw):
    rows = src.shape[0]
    n_tiles = rows // MOE_TILE
    grid_spec = pltpu.PrefetchScalarGridSpec(
        num_scalar_prefetch=2,
        grid=(n_tiles,),
        in_specs=[
            pl.BlockSpec(memory_space=pl.ANY),
            pl.BlockSpec((None, D_MODEL, D_FF), lambda i, te, sr: (te[i], 0, 0)),
            pl.BlockSpec((None, D_MODEL, D_FF), lambda i, te, sr: (te[i], 0, 0)),
            pl.BlockSpec((None, D_FF, D_MODEL), lambda i, te, sr: (te[i], 0, 0)),
        ],
        out_specs=pl.BlockSpec((MOE_TILE, D_MODEL), lambda i, te, sr: (i, 0)),
        scratch_shapes=[pltpu.VMEM((MOE_TILE, D_MODEL), F32), pltpu.VMEM((MOE_TILE, D_MODEL), F32),
                        pltpu.SemaphoreType.DMA((2,))],
    )
    return pl.pallas_call(
        functools.partial(_moe_kernel, n_tiles=n_tiles),
        grid_spec=grid_spec,
        out_shape=jax.ShapeDtypeStruct((rows, D_MODEL), BF16),
        compiler_params=_cparams("arbitrary"),
        name="moe_experts",
    )(tile_expert, src, h, lw["w_gate_ff"], lw["w_up_ff"], lw["w_down_ff"])


def _moe_plan(ids, rank, counts):
    two_t = ids.shape[0] * ids.shape[1]
    t = ids.shape[1]
    counts = counts[:, 0].astype(I32)
    padded = ((counts + MOE_TILE - 1) // MOE_TILE) * MOE_TILE
    ends = jnp.cumsum(padded)
    starts = ends - padded
    onehot = ids[:, :, None] == jnp.arange(N_EXPERTS, dtype=I32)[None, None, :]
    pos = (jnp.sum(jnp.where(onehot, starts[None, None, :], 0), axis=2) + rank).reshape(two_t)
    n_rows = two_t + N_EXPERTS * MOE_TILE
    n_tiles = n_rows // MOE_TILE
    token = jnp.arange(two_t, dtype=I32) % t
    src = jnp.zeros((n_rows,), I32).at[pos].set(token, mode="promise_in_bounds", unique_indices=True)
    tile_start = jnp.arange(n_tiles, dtype=I32) * MOE_TILE
    tile_expert = jnp.minimum(jnp.sum((tile_start[:, None] >= ends[None, :]).astype(I32), axis=1), N_EXPERTS - 1)
    return pos, src, tile_expert.astype(I32)


def moe_ffn(h2, ids, rank, counts, lw):
    pos, src, tile_expert = _moe_plan(ids, rank, counts)
    ys = moe_experts(h2, src, tile_expert, lw)
    return ys.at[pos].get(mode="promise_in_bounds", unique_indices=True)


def _rope_tables(n_tokens):
    rows = n_tokens // GRID_W
    row = jnp.repeat(jnp.arange(rows, dtype=F32), GRID_W)
    col = jnp.tile(jnp.arange(GRID_W, dtype=F32), rows)
    n_freq = HEAD_DIM // 4
    inv = ROPE_THETA ** (-jnp.arange(n_freq, dtype=F32) / n_freq)
    ang = jnp.concatenate([row[:, None] * inv, col[:, None] * inv], axis=-1)
    cos = jnp.repeat(jnp.cos(ang), 2, axis=-1)
    sin = jnp.repeat(jnp.sin(ang), 2, axis=-1)
    sign = jnp.tile(jnp.array([-1.0, 1.0], F32), HEAD_DIM // 2)
    sin = sin * sign
    return cos, sin


def _prep_layer_weights(l, p):
    w_in = p["w_in"][l]
    gates = w_in[:, _O_GATE:]
    w_main = jnp.concatenate([gates, w_in[:, _O_Z:_O_XBC], w_in[:, _O_Q:_O_K], w_in[:, _O_PIN:_O_GATE],
                              w_in[:, _O_XBC:_O_DTF]], axis=1).astype(BF16)
    w_small = jnp.concatenate([w_in[:, _O_K:_O_PIN], w_in[:, _O_DTF:_O_Q],
                               jnp.zeros((D_MODEL, LANES - 2 * SSD_HEADS), F32)], axis=1).astype(BF16)
    pad_lanes = lambda v: jnp.concatenate([v.reshape(-1), jnp.zeros((LANES - v.size,), F32)]).reshape(1, LANES)
    return {
        "w_main": w_main,
        "w_small": w_small,
        "conv_w": jnp.concatenate([p["conv_w"][l], jnp.zeros((8 - CONV_W, XBC_DIM), F32)], axis=0),
        "conv_b": p["conv_b"][l].reshape(1, XBC_DIM),
        "dt_bias": pad_lanes(p["dt_bias"][l]),
        "a_log": pad_lanes(p["a_log"][l]),
        "d_skip": jnp.repeat(p["d_skip"][l], SSD_HEADDIM).reshape(1, SSD_INNER),
        "ssd_norm": p["ssd_norm"][l].reshape(1, SSD_INNER),
        "q_gain": jnp.tile(p["q_norm"][l], LANES // HEAD_DIM).reshape(1, LANES),
        "k_gain": jnp.tile(p["k_norm"][l], KV_HEADS).reshape(1, KV_INNER),
        "pool_w": p["pool_w"][l].astype(BF16),
        "pool_scale": p["pool_scale"][l].reshape(1, POOL_INNER),
        "w_ssd_out": p["w_ssd_out"][l].astype(BF16),
        "w_attn_out": p["w_attn_out"][l].astype(BF16),
        "w_pool_out": p["w_pool_out"][l].astype(BF16),
        "w_out": p["w_out"][l].astype(BF16),
        "norm_mix": p["norm_mix"][l].reshape(1, D_MODEL),
        "norm_ffn": p["norm_ffn"][l].reshape(1, D_MODEL),
        "w_gate_ff": p["w_gate_ff"][l].astype(BF16),
        "w_up_ff": p["w_up_ff"][l].astype(BF16),
        "w_down_ff": p["w_down_ff"][l].astype(BF16),
    }


def _mixer_stage(h, x, mods, lw, gw, path, n_seq, seq_len, tq, h0, rope, cache):
    t = n_seq * seq_len
    main = matmul(h, lw["w_main"], BF16, min(t, 1024), 1536, "in_proj_main")
    small = matmul(h, lw["w_small"], F32, min(t, 1024), SMALL_N, "in_proj_small")
    y_ssd, hf, hb = ssd_mixer(main, small, h0[0], h0[1], lw, n_seq, seq_len)
    if rope is None:
        k_norm, kt, v4 = kv_prep(small, lw["k_gain"], n_seq, seq_len)
        o_attn = attention(main, lw["q_gain"], kt, v4, n_seq, seq_len, tq)
    else:
        k_norm = None
        kt, v4 = kv_prep(small, lw["k_gain"], n_seq, seq_len, rope_kv=rope[0], cache=cache)
        o_attn = attention(main, lw["q_gain"], kt, v4, n_seq, seq_len, tq, rope_q=rope[1])
    o_pool = pool_mixer(main, lw["pool_w"], lw["pool_scale"], n_seq, seq_len)
    merged = merge_branches(y_ssd, o_attn, o_pool, main, lw)
    x_new, h2, ids, rank, wcol, counts = outproj_route(x, merged, mods, lw, gw, path)
    y2 = moe_ffn(h2, ids, rank, counts, lw)
    return x_new, y2, wcol, k_norm, small, hf, hb


def kernel(x_prompt, x_sample, cache_k, cache_v, state_ssd_fwd, state_ssd_bwd, c, c_ctx, w_ada, b_ada, norm_mix,
           norm_ffn, w_in, conv_w, conv_b, a_log, dt_bias, d_skip, ssd_norm, q_norm, k_norm, pool_w, pool_scale,
           w_ssd_out, w_attn_out, w_pool_out, w_out, w_router, b_router, w_gate_ff, w_up_ff, w_down_ff, norm_final):
    p = dict(norm_mix=norm_mix, norm_ffn=norm_ffn, w_in=w_in, conv_w=conv_w, conv_b=conv_b, a_log=a_log,
             dt_bias=dt_bias, d_skip=d_skip, ssd_norm=ssd_norm, q_norm=q_norm, k_norm=k_norm, pool_w=pool_w,
             pool_scale=pool_scale, w_ssd_out=w_ssd_out, w_attn_out=w_attn_out, w_pool_out=w_pool_out, w_out=w_out,
             w_gate_ff=w_gate_ff, w_up_ff=w_up_ff, w_down_ff=w_down_ff)
    bp, lc, _ = x_prompt.shape
    depth = w_in.shape[0]
    bs, ls, _ = x_sample.shape
    past = cache_k.shape[2]
    tc, tl = bp * lc, bs * ls

    mod_rows = -(-(1 + bs) // 8) * 8
    cond = jnp.concatenate([c_ctx[None, :], c, jnp.zeros((mod_rows - 1 - bs, D_MODEL), F32)], axis=0)
    mods_all = ada_mods(cond, w_ada, b_ada).reshape(depth, mod_rows, 6, 1, D_MODEL)

    wr_hi, wr_lo = _split_bf16(w_router)
    zpad = jnp.zeros((D_MODEL, LANES - 2 * N_EXPERTS), BF16)
    gw = {
        "w_router_cat": jnp.concatenate([wr_hi, wr_lo, zpad], axis=1),
        "w_router_hi": jnp.concatenate([wr_hi, jnp.zeros((D_MODEL, LANES - N_EXPERTS), BF16)], axis=1),
        "b_router": b_router.reshape(N_EXPERTS, 1),
    }
    cos, sin = _rope_tables(ls)
    rope = ((jnp.tile(cos, (1, KV_HEADS)), jnp.tile(sin, (1, KV_HEADS))),
            (jnp.tile(cos, (1, LANES // HEAD_DIM)), jnp.tile(sin, (1, LANES // HEAD_DIM))))
    cache_k4 = cache_k.reshape(bs, depth, past, KV_INNER)
    cache_v4 = cache_v.reshape(bs, depth, past, KV_INNER)
    pairs = SSD_HEADS // 2

    def pack_state(st):
        st = st.reshape(bs, depth, pairs, 2, SSD_HEADDIM, D_STATE)
        return st.transpose(0, 1, 2, 5, 3, 4).reshape(bs, depth, pairs, D_STATE, 2 * SSD_HEADDIM)

    def unpack_state(st):
        st = st.reshape(st.shape[0], pairs, D_STATE, 2, SSD_HEADDIM)
        return st.transpose(0, 1, 3, 4, 2).reshape(st.shape[0], SSD_HEADS, SSD_HEADDIM, D_STATE)

    st_f = pack_state(state_ssd_fwd)
    st_b = pack_state(state_ssd_bwd)

    path_c = (0, False, lc)
    path_l = (1, True, ls)
    xc = x_prompt.reshape(tc, D_MODEL)
    xl = x_sample.reshape(tl, D_MODEL)
    ks, vs, hfs, hbs = [], [], [], []
    hc = hl = None
    y2c = y2l = wc = wl = None
    for l in range(depth):
        lw = _prep_layer_weights(l, p)
        mods = mods_all[l]
        if l == 0:
            hc = modulate_first(xc, mods, lw["norm_mix"], path_c)
            hl = modulate_first(xl, mods, lw["norm_mix"], path_l)
        else:
            xc, hc = residual_modulate(xc, y2c, wc, mods_all[l - 1], mods, lw["norm_mix"], path_c)
            xl, hl = residual_modulate(xl, y2l, wl, mods_all[l - 1], mods, lw["norm_mix"], path_l)
        xc, y2c, wc, k_c, small_c, hf, hb = _mixer_stage(
            hc, xc, mods, lw, gw, path_c, bp, lc, lc, (None, None), None, None)
        ks.append(k_c.reshape(bp, lc, KV_HEADS, HEAD_DIM))
        vs.append(small_c[:, KV_INNER:2 * KV_INNER].reshape(bp, lc, KV_HEADS, HEAD_DIM))
        hfs.append(unpack_state(hf))
        hbs.append(unpack_state(hb))
        xl, y2l, wl, _, _, _, _ = _mixer_stage(
            hl, xl, mods, lw, gw, path_l, bs, ls, CHUNK, (st_f[:, l], st_b[:, l]), rope, (cache_k4, cache_v4, l))
    gain_f = norm_final.reshape(1, D_MODEL)
    y_prompt = residual_final(xc, y2c, wc, mods_all[depth - 1], gain_f, path_c).reshape(bp, lc, D_MODEL)
    y_sample = residual_final(xl, y2l, wl, mods_all[depth - 1], gain_f, path_l).reshape(bs, ls, D_MODEL)
    return (y_prompt, y_sample, jnp.stack(ks, axis=1), jnp.stack(vs, axis=1), jnp.stack(hfs, axis=1),
            jnp.stack(hbs, axis=1))
```

```python
import functools

import jax
import jax.numpy as jnp
from jax import lax
from jax.experimental import pallas as pl
from jax.experimental.pallas import tpu as pltpu

F32 = jnp.float32
BF16 = jnp.bfloat16
I32 = jnp.int32

D_MODEL = 2048
GRID_W = 64
EPS = 1e-6

SSD_HEADS = 16
SSD_HEADDIM = 64
SSD_INNER = SSD_HEADS * SSD_HEADDIM
SSD_GROUPS = 4
D_STATE = 64
CONV_W = 5
CHUNK = 128
XBC_DIM = SSD_INNER + 2 * SSD_GROUPS * D_STATE

N_HEADS = 16
KV_HEADS = 4
HEAD_DIM = 64
ATTN_INNER = N_HEADS * HEAD_DIM
KV_INNER = KV_HEADS * HEAD_DIM
ROPE_THETA = 10000.0

POOL_GROUPS = 4
POOL_INNER = 1024
POOL_GROUP_DIM = POOL_INNER // POOL_GROUPS
POOL_WINDOWS = (2, 4, 8, 16)

N_EXPERTS = 16
N_EXPERT_GROUPS = 4
EXPERTS_PER_GROUP = N_EXPERTS // N_EXPERT_GROUPS
D_FF = 512

_O_Z = 0
_O_XBC = _O_Z + SSD_INNER
_O_DTF = _O_XBC + XBC_DIM
_O_DTB = _O_DTF + SSD_HEADS
_O_Q = _O_DTB + SSD_HEADS
_O_K = _O_Q + ATTN_INNER
_O_V = _O_K + KV_INNER
_O_PIN = _O_V + KV_INNER
_O_GATE = _O_PIN + POOL_INNER

MAIN_N = 3 * D_MODEL + SSD_INNER + ATTN_INNER + POOL_INNER + XBC_DIM
LANES = 128
SMALL_N = 2 * KV_INNER + LANES

HALO = 16
MOE_TILE = 256
ROUTE_TILE = 512
ROW_TILES = D_MODEL // LANES
VMEM_LIMIT = 56 * 1024 * 1024


def _cparams(*sem):
    return pltpu.CompilerParams(dimension_semantics=sem, vmem_limit_bytes=VMEM_LIMIT)


def _silu(x):
    return x * jax.nn.sigmoid(x)


def _rms(xf):
    return xf * lax.rsqrt(jnp.mean(xf * xf, axis=-1, keepdims=True) + EPS)


def _split_bf16(x):
    hi = x.astype(BF16)
    lo = (x - hi.astype(F32)).astype(BF16)
    return hi, lo


def _ada_kernel(cond_ref, w_ref, b_ref, o_ref):
    c = _silu(cond_ref[...])
    o_ref[...] = jnp.dot(c.astype(BF16), w_ref[...].astype(BF16), preferred_element_type=F32) + b_ref[...]


def ada_mods(cond, w_ada, b_ada, tn=1024):
    rows = cond.shape[0]
    depth, _, n = w_ada.shape
    return pl.pallas_call(
        _ada_kernel,
        grid=(depth, n // tn),
        in_specs=[
            pl.BlockSpec((rows, D_MODEL), lambda l, j: (0, 0)),
            pl.BlockSpec((None, D_MODEL, tn), lambda l, j: (l, 0, j)),
            pl.BlockSpec((None, 1, tn), lambda l, j: (l, 0, j)),
        ],
        out_specs=pl.BlockSpec((None, rows, tn), lambda l, j: (l, 0, j)),
        out_shape=jax.ShapeDtypeStruct((depth, rows, n), F32),
        compiler_params=_cparams("parallel", "parallel"),
        name="ada_mods",
    )(cond, w_ada, b_ada.reshape(depth, 1, n))


def _mod_spec(which, row_of_tile):
    return pl.BlockSpec((None, None, 1, D_MODEL), lambda i: (row_of_tile(i), which, 0, 0))


def _row_fn(path, tm):
    row0, per_seq, seq_len = path
    if per_seq:
        return lambda i: row0 + (i * tm) // seq_len
    return lambda i: row0


def _modulate_kernel(x_ref, g_ref, sh_ref, sc_ref, h_ref):
    xf = x_ref[...]
    h_ref[...] = (_rms(xf) * g_ref[...] * (1.0 + sc_ref[...]) + sh_ref[...]).astype(h_ref.dtype)


def modulate_first(x, mods, gain, path, tm=512):
    t = x.shape[0]
    rf = _row_fn(path, tm)
    return pl.pallas_call(
        _modulate_kernel,
        grid=(t // tm,),
        in_specs=[
            pl.BlockSpec((tm, D_MODEL), lambda i: (i, 0)),
            pl.BlockSpec((1, D_MODEL), lambda i: (0, 0)),
            _mod_spec(0, rf),
            _mod_spec(1, rf),
        ],
        out_specs=pl.BlockSpec((tm, D_MODEL), lambda i: (i, 0)),
        out_shape=jax.ShapeDtypeStruct((t, D_MODEL), BF16),
        compiler_params=_cparams("parallel"),
        name="modulate_first",
    )(x, gain, mods, mods)


def _moe_residual(x_ref, ya_ref, yb_ref, w_ref, g2_ref):
    w = w_ref[...]
    y = w[:, 0:1] * ya_ref[...].astype(F32) + w[:, 1:2] * yb_ref[...].astype(F32)
    return x_ref[...] + g2_ref[...] * y


def _residual_modulate_kernel(x_ref, ya_ref, yb_ref, w_ref, g2_ref, gain_ref, sh_ref, sc_ref, xo_ref, h_ref):
    xn = _moe_residual(x_ref, ya_ref, yb_ref, w_ref, g2_ref)
    xo_ref[...] = xn
    h_ref[...] = (_rms(xn) * gain_ref[...] * (1.0 + sc_ref[...]) + sh_ref[...]).astype(h_ref.dtype)


def residual_modulate(x, y2, wcol, mods_prev, mods_next, gain_next, path):
    t = x.shape[0]
    tm = ROUTE_TILE
    nt = t // tm
    rf = _row_fn(path, tm)
    return pl.pallas_call(
        _residual_modulate_kernel,
        grid=(nt,),
        in_specs=[
            pl.BlockSpec((tm, D_MODEL), lambda i: (i, 0)),
            pl.BlockSpec((tm, D_MODEL), lambda i: (2 * i, 0)),
            pl.BlockSpec((tm, D_MODEL), lambda i: (2 * i + 1, 0)),
            pl.BlockSpec((tm, LANES), lambda i: (i, 0)),
            _mod_spec(5, rf),
            pl.BlockSpec((1, D_MODEL), lambda i: (0, 0)),
            _mod_spec(0, rf),
            _mod_spec(1, rf),
        ],
        out_specs=[
            pl.BlockSpec((tm, D_MODEL), lambda i: (i, 0)),
            pl.BlockSpec((tm, D_MODEL), lambda i: (i, 0)),
        ],
        out_shape=[
            jax.ShapeDtypeStruct((t, D_MODEL), F32),
            jax.ShapeDtypeStruct((t, D_MODEL), BF16),
        ],
        compiler_params=_cparams("parallel"),
        name="residual_modulate",
    )(x, y2, y2, wcol, mods_prev, gain_next, mods_next, mods_next)


def _residual_final_kernel(x_ref, ya_ref, yb_ref, w_ref, g2_ref, gain_ref, y_ref):
    xn = _moe_residual(x_ref, ya_ref, yb_ref, w_ref, g2_ref)
    y_ref[...] = _rms(xn) * gain_ref[...]


def residual_final(x, y2, wcol, mods_prev, gain, path):
    t = x.shape[0]
    tm = ROUTE_TILE
    nt = t // tm
    rf = _row_fn(path, tm)
    return pl.pallas_call(
        _residual_final_kernel,
        grid=(nt,),
        in_specs=[
            pl.BlockSpec((tm, D_MODEL), lambda i: (i, 0)),
            pl.BlockSpec((tm, D_MODEL), lambda i: (2 * i, 0)),
            pl.BlockSpec((tm, D_MODEL), lambda i: (2 * i + 1, 0)),
            pl.BlockSpec((tm, LANES), lambda i: (i, 0)),
            _mod_spec(5, rf),
            pl.BlockSpec((1, D_MODEL), lambda i: (0, 0)),
        ],
        out_specs=pl.BlockSpec((tm, D_MODEL), lambda i: (i, 0)),
        out_shape=jax.ShapeDtypeStruct((t, D_MODEL), F32),
        compiler_params=_cparams("parallel"),
        name="residual_final",
    )(x, y2, y2, wcol, mods_prev, gain)


def _mm_kernel(a_ref, b_ref, o_ref):
    o_ref[...] = jnp.dot(a_ref[...], b_ref[...], preferred_element_type=F32).astype(o_ref.dtype)


def matmul(a, b, layer, out_dtype, tm, tn, name):
    m, k = a.shape
    n = b.shape[2]
    return pl.pallas_call(
        _mm_kernel,
        grid=(m // tm, n // tn),
        in_specs=[
            pl.BlockSpec((tm, k), lambda i, j: (i, 0)),
            pl.BlockSpec((None, k, tn), lambda i, j: (layer, 0, j)),
        ],
        out_specs=pl.BlockSpec((tm, tn), lambda i, j: (i, j)),
        out_shape=jax.ShapeDtypeStruct((m, n), out_dtype),
        compiler_params=_cparams("parallel", "arbitrary"),
        name=name,
    )(a, b)


def _chunk_window(ref, c, n_chunks, seq_len):
    s = pl.multiple_of(c * CHUNK, CHUNK)
    cur = ref[pl.ds(s, CHUNK), :].astype(F32)
    sp = pl.multiple_of(jnp.maximum(s - HALO, 0), HALO)
    sn = pl.multiple_of(jnp.minimum(s + CHUNK, seq_len - HALO), HALO)
    prev = ref[pl.ds(sp, HALO), :].astype(F32)
    nxt = ref[pl.ds(sn, HALO), :].astype(F32)
    prev = jnp.where(c > 0, prev, 0.0)
    nxt = jnp.where(c < n_chunks - 1, nxt, 0.0)
    return jnp.concatenate([prev, cur, nxt], axis=0)


def _ssd_kernel(*refs, seq_len, has_h0):
    if has_h0:
        (xbc_ref, z_ref, dt_ref, h0f_ref, h0b_ref, cw_ref, cb_ref, dtb_ref, alog_ref, dskip_ref, gain_ref,
         expand_ref, colsel_ref, y_ref, hf_ref, hb_ref, xc_ref, sp_ref, yacc_ref, st_ref) = refs
    else:
        (xbc_ref, z_ref, dt_ref, cw_ref, cb_ref, dtb_ref, alog_ref, dskip_ref, gain_ref,
         expand_ref, colsel_ref, y_ref, hf_ref, hb_ref, xc_ref, sp_ref, yacc_ref, st_ref) = refs
        h0f_ref = h0b_ref = None
    n_chunks = seq_len // CHUNK
    win = CHUNK + 2 * HALO

    def conv_chunk(c, carry):
        w = _chunk_window(xbc_ref, c, n_chunks, seq_len)
        acc = jnp.zeros((CHUNK, XBC_DIM), F32) + cb_ref[...]
        for j in range(CONV_W):
            shift = (CONV_W // 2 - j) % win
            wj = w if shift == 0 else pltpu.roll(w, shift, 0)
            acc = acc + cw_ref[j:j + 1, :] * wj[HALO:HALO + CHUNK, :]
        s = pl.multiple_of(c * CHUNK, CHUNK)
        xc_ref[pl.ds(s, CHUNK), :] = _silu(acc).astype(BF16)
        return carry

    lax.fori_loop(0, n_chunks, conv_chunk, 0)

    xdt = dt_ref[...] + dtb_ref[...]
    sp_ref[...] = jnp.maximum(xdt, 0.0) + jnp.log1p(jnp.exp(-jnp.abs(xdt)))
    a_row = -jnp.exp(alog_ref[...])

    rows = lax.broadcasted_iota(I32, (CHUNK, CHUNK), 0)
    cols = lax.broadcasted_iota(I32, (CHUNK, CHUNK), 1)
    lane_lo = cols < SSD_HEADDIM

    def cat2(x):
        hi, lo = _split_bf16(x)
        return jnp.concatenate([hi, lo], axis=1)

    def scan_chunk(c, forward):
        d = 0 if forward else 1
        off = d * SSD_HEADS
        s = pl.multiple_of(c * CHUNK, CHUNK)
        xcv = xc_ref[pl.ds(s, CHUNK), :]
        dtc = sp_ref[pl.ds(s, CHUNK), :]
        a = dtc * a_row
        tri = (rows >= cols) if forward else (cols >= rows)
        tri_b = jnp.where(tri, 1.0, 0.0).astype(BF16)
        a1 = a.astype(BF16)
        r1 = a - a1.astype(F32)
        a2 = r1.astype(BF16)
        a3 = (r1 - a2.astype(F32)).astype(BF16)
        cs = (jnp.dot(tri_b, a1, preferred_element_type=F32) + jnp.dot(tri_b, a2, preferred_element_type=F32)
              + jnp.dot(tri_b, a3, preferred_element_type=F32))
        cs_t = cs.T
        tot = cs[CHUNK - 1:CHUNK, :] if forward else cs[0:1, :]
        expand = expand_ref[d]
        dt_full = jnp.dot(cat2(dtc), expand, preferred_element_type=F32)
        dout_full = jnp.dot(cat2(jnp.exp(cs)), expand, preferred_element_type=F32)
        dst_full = jnp.dot(cat2(jnp.exp(tot - cs)), expand, preferred_element_type=F32)
        etot_full = dout_full[CHUNK - 1:CHUNK, :] if forward else dout_full[0:1, :]
        colb = jnp.dot(cat2(cs), colsel_ref[d], preferred_element_type=F32)
        xd = xcv[:, :SSD_INNER].astype(F32) * dt_full
        xdb = xd.astype(BF16)
        xdd = (xd * dst_full).astype(BF16)
        b_t = xcv[:, SSD_INNER:SSD_INNER + SSD_GROUPS * D_STATE].astype(F32).T.astype(BF16)
        zero = jnp.zeros((CHUNK, CHUNK), BF16)
        ys = []
        for g in range(SSD_GROUPS):
            bg_t = b_t[g * D_STATE:(g + 1) * D_STATE, :]
            cg = xcv[:, SSD_INNER + (SSD_GROUPS + g) * D_STATE:SSD_INNER + (SSD_GROUPS + g + 1) * D_STATE]
            gmat = jnp.dot(cg, bg_t, preferred_element_type=F32)
            for pp in range(2):
                j = 2 * g + pp
                ms = []
                for hh in range(2):
                    h = 2 * j + hh
                    diff = colb[:, h * CHUNK:(h + 1) * CHUNK] - cs_t[off + h:off + h + 1, :]
                    lm = jnp.exp(jnp.where(tri, diff, -jnp.inf))
                    ms.append((gmat * lm).astype(BF16))
                mcat = jnp.concatenate(ms, axis=1)
                xp = xdb[:, j * CHUNK:(j + 1) * CHUNK]
                xblk = jnp.concatenate([jnp.where(lane_lo, xp, zero), jnp.where(lane_lo, zero, xp)], axis=0)
                y_diag = jnp.dot(mcat, xblk, preferred_element_type=F32)
                st = st_ref[j]
                y_off = jnp.dot(cg, st.astype(BF16), preferred_element_type=F32) * dout_full[:, j * CHUNK:(j + 1) * CHUNK]
                contrib = jnp.dot(bg_t, xdd[:, j * CHUNK:(j + 1) * CHUNK], preferred_element_type=F32)
                st_ref[j] = st * etot_full[:, j * CHUNK:(j + 1) * CHUNK] + contrib
                ys.append(y_diag + y_off)
        return s, xcv, jnp.concatenate(ys, axis=1)

    def fwd_chunk(c, carry):
        s, _, y = scan_chunk(c, True)
        yacc_ref[pl.ds(s, CHUNK), :] = y
        return carry

    def bwd_chunk(i, carry):
        c = n_chunks - 1 - i
        s, xcv, y = scan_chunk(c, False)
        xs = xcv[:, :SSD_INNER].astype(F32)
        y = yacc_ref[pl.ds(s, CHUNK), :] + y + dskip_ref[...] * xs
        y = y * _silu(z_ref[pl.ds(s, CHUNK), :].astype(F32))
        y_ref[pl.ds(s, CHUNK), :] = (_rms(y) * gain_ref[...]).astype(y_ref.dtype)
        return carry

    def store_state(out_ref):
        for j in range(SSD_HEADS // 2):
            sq = jnp.concatenate([st_ref[j], jnp.zeros((2 * SSD_HEADDIM - D_STATE, 2 * SSD_HEADDIM), F32)], axis=0)
            out_ref[j] = sq.T[:, :D_STATE]

    if has_h0:
        st_ref[...] = h0f_ref[...]
    else:
        st_ref[...] = jnp.zeros_like(st_ref)
    lax.fori_loop(0, n_chunks, fwd_chunk, 0)
    store_state(hf_ref)
    if has_h0:
        st_ref[...] = h0b_ref[...]
    else:
        st_ref[...] = jnp.zeros_like(st_ref)
    lax.fori_loop(0, n_chunks, bwd_chunk, 0)
    store_state(hb_ref)


def ssd_mixer(main, small, h0f, h0b, lw, n_seq, seq_len):
    t = n_seq * seq_len
    has_h0 = h0f is not None
    pairs = SSD_HEADS // 2
    st_shape = (pairs, D_STATE, 2 * SSD_HEADDIM)
    out_st_shape = (pairs, 2 * SSD_HEADDIM, D_STATE)
    st_spec = pl.BlockSpec((None,) + st_shape, lambda b: (b, 0, 0, 0))
    out_st_spec = pl.BlockSpec((None,) + out_st_shape, lambda b: (b, 0, 0, 0))
    full = lambda shape: pl.BlockSpec(shape, lambda b: (0,) * len(shape))
    in_specs = [
        pl.BlockSpec((seq_len, XBC_DIM), lambda b: (b, (MAIN_N - XBC_DIM) // XBC_DIM)),
        pl.BlockSpec((seq_len, SSD_INNER), lambda b: (b, 3 * D_MODEL // SSD_INNER)),
        pl.BlockSpec((seq_len, LANES), lambda b: (b, 2 * KV_INNER // LANES)),
    ]
    args = [main, main, small]
    if has_h0:
        in_specs += [st_spec, st_spec]
        args += [h0f, h0b]
    k_head = (jnp.arange(2 * LANES, dtype=I32) % LANES)[None, :, None] - SSD_HEADS * jnp.arange(2, dtype=I32)[:, None, None]
    expand = (k_head == (jnp.arange(SSD_INNER, dtype=I32) // SSD_HEADDIM)[None, None, :]).astype(BF16)
    colsel = (k_head == (jnp.arange(SSD_HEADS * CHUNK, dtype=I32) // CHUNK)[None, None, :]).astype(BF16)
    in_specs += [full((8, XBC_DIM)), full((1, XBC_DIM)), full((1, LANES)), full((1, LANES)),
                 full((1, SSD_INNER)), full((1, SSD_INNER)), full((2, 2 * LANES, SSD_INNER)),
                 full((2, 2 * LANES, SSD_HEADS * CHUNK))]
    args += [lw["conv_w"], lw["conv_b"], lw["dt_bias"], lw["a_log"], lw["d_skip"], lw["ssd_norm"], expand, colsel]
    return pl.pallas_call(
        functools.partial(_ssd_kernel, seq_len=seq_len, has_h0=has_h0),
        grid=(n_seq,),
        in_specs=in_specs,
        out_specs=[pl.BlockSpec((seq_len, SSD_INNER), lambda b: (b, 0)), out_st_spec, out_st_spec],
        out_shape=[
            jax.ShapeDtypeStruct((t, SSD_INNER), BF16),
            jax.ShapeDtypeStruct((n_seq,) + out_st_shape, F32),
            jax.ShapeDtypeStruct((n_seq,) + out_st_shape, F32),
        ],
        scratch_shapes=[
            pltpu.VMEM((seq_len, XBC_DIM), BF16),
            pltpu.VMEM((seq_len, LANES), F32),
            pltpu.VMEM((seq_len, SSD_INNER), F32),
            pltpu.VMEM(st_shape, F32),
        ],
        compiler_params=_cparams("parallel"),
        name="ssd_mixer",
    )(*args)


def _swap_pairs(x):
    n = x.shape[-1]
    lane = lax.broadcasted_iota(I32, x.shape, x.ndim - 1)
    return jnp.where(lane % 2 == 0, pltpu.roll(x, n - 1, x.ndim - 1), pltpu.roll(x, 1, x.ndim - 1))


def _kv_prep_kernel(*refs, seq_len, rope):
    if rope:
        (k_ref, v_ref, gain_ref, cos_ref, sin_ref, ck_ref, cv_ref, kt_ref, v4_ref) = refs
    else:
        (k_ref, v_ref, gain_ref, kn_ref, kt_ref, v4_ref) = refs
    kf = k_ref[...]
    parts = []
    for g in range(KV_HEADS):
        kh = kf[:, g * HEAD_DIM:(g + 1) * HEAD_DIM]
        parts.append(kh * lax.rsqrt(jnp.mean(kh * kh, axis=-1, keepdims=True) + EPS))
    kn = jnp.concatenate(parts, axis=1) * gain_ref[...]
    if rope:
        kn = kn * cos_ref[...] + _swap_pairs(kn) * sin_ref[...]
    else:
        kn_ref[...] = kn
    kt_ref[:, 0:seq_len] = kn.T.astype(BF16)
    def with_ones(vh):
        return jnp.concatenate([vh, jnp.ones_like(vh)], axis=1).astype(BF16)

    vf = v_ref[...]
    for g in range(KV_HEADS):
        v4_ref[g, 0:seq_len, :] = with_ones(vf[:, g * HEAD_DIM:(g + 1) * HEAD_DIM])
    if rope:
        past = ck_ref.shape[0]
        kt_ref[:, seq_len:seq_len + past] = ck_ref[...].T.astype(BF16)
        cvf = cv_ref[...]
        for g in range(KV_HEADS):
            v4_ref[g, seq_len:seq_len + past, :] = with_ones(cvf[:, g * HEAD_DIM:(g + 1) * HEAD_DIM])


def kv_prep(small, k_gain, n_seq, seq_len, rope_kv=None, cache=None):
    t = n_seq * seq_len
    rope = rope_kv is not None
    in_specs = [
        pl.BlockSpec((seq_len, KV_INNER), lambda b: (b, 0)),
        pl.BlockSpec((seq_len, KV_INNER), lambda b: (b, 1)),
        pl.BlockSpec((1, KV_INNER), lambda b: (0, 0)),
    ]
    args = [small, small, k_gain]
    n_keys = seq_len
    out_specs, out_shape = [], []
    if rope:
        cache_k, cache_v, layer = cache
        past = cache_k.shape[2]
        n_keys += past
        in_specs += [
            pl.BlockSpec((seq_len, KV_INNER), lambda b: (0, 0)),
            pl.BlockSpec((seq_len, KV_INNER), lambda b: (0, 0)),
            pl.BlockSpec((None, None, past, KV_INNER), lambda b: (b, layer, 0, 0)),
            pl.BlockSpec((None, None, past, KV_INNER), lambda b: (b, layer, 0, 0)),
        ]
        args += [rope_kv[0], rope_kv[1], cache_k, cache_v]
    else:
        out_specs.append(pl.BlockSpec((seq_len, KV_INNER), lambda b: (b, 0)))
        out_shape.append(jax.ShapeDtypeStruct((t, KV_INNER), F32))
    out_specs += [
        pl.BlockSpec((None, KV_INNER, n_keys), lambda b: (b, 0, 0)),
        pl.BlockSpec((None, KV_HEADS, n_keys, 2 * HEAD_DIM), lambda b: (b, 0, 0, 0)),
    ]
    out_shape += [
        jax.ShapeDtypeStruct((n_seq, KV_INNER, n_keys), BF16),
        jax.ShapeDtypeStruct((n_seq, KV_HEADS, n_keys, 2 * HEAD_DIM), BF16),
    ]
    return pl.pallas_call(
        functools.partial(_kv_prep_kernel, seq_len=seq_len, rope=rope),
        grid=(n_seq,),
        in_specs=in_specs,
        out_specs=out_specs,
        out_shape=out_shape,
        compiler_params=_cparams("parallel"),
        name="kv_prep",
    )(*args)


def _attn_kernel(*refs, rope, tq):
    if rope:
        q_ref, gain_ref, cos_ref, sin_ref, kt_ref, v4_ref, o_ref = refs
    else:
        q_ref, gain_ref, kt_ref, v4_ref, o_ref = refs
    rep = N_HEADS // KV_HEADS
    scale = HEAD_DIM ** -0.5
    heads = []
    for j in range(N_HEADS // 2):
        qs = q_ref[:, j * LANES:(j + 1) * LANES].astype(F32)
        qg = qs * gain_ref[...]
        if rope:
            qg = qg * cos_ref[...] + _swap_pairs(qg) * sin_ref[...]
        for hh in range(2):
            raw = qs[:, hh * HEAD_DIM:(hh + 1) * HEAD_DIM]
            inv = lax.rsqrt(jnp.mean(raw * raw, axis=-1, keepdims=True) + EPS) * scale
            heads.append((qg[:, hh * HEAD_DIM:(hh + 1) * HEAD_DIM] * inv).astype(BF16))
    for g in range(KV_HEADS):
        qstack = jnp.concatenate(heads[g * rep:(g + 1) * rep], axis=0)
        kt = kt_ref[g * HEAD_DIM:(g + 1) * HEAD_DIM, :]
        s = jnp.dot(qstack, kt, preferred_element_type=F32)
        m = jnp.max(s, axis=-1, keepdims=True)
        p = jnp.exp((s - m).astype(BF16))
        ov = jnp.dot(p, v4_ref[g], preferred_element_type=F32)
        o = ov[:, :HEAD_DIM] / ov[:, HEAD_DIM:HEAD_DIM + 1]
        for r in range(rep):
            h = g * rep + r
            o_ref[:, h * HEAD_DIM:(h + 1) * HEAD_DIM] = o[r * tq:(r + 1) * tq, :].astype(o_ref.dtype)


def attention(main, q_gain, kt, v4, n_seq, seq_len, tq, rope_q=None):
    t = n_seq * seq_len
    nq = seq_len // tq
    n_keys = kt.shape[2]
    rope = rope_q is not None
    in_specs = [
        pl.BlockSpec((tq, ATTN_INNER), lambda b, i: (b * nq + i, (3 * D_MODEL + SSD_INNER) // ATTN_INNER)),
        pl.BlockSpec((1, LANES), lambda b, i: (0, 0)),
    ]
    args = [main, q_gain]
    if rope:
        in_specs += [pl.BlockSpec((tq, LANES), lambda b, i: (i, 0)), pl.BlockSpec((tq, LANES), lambda b, i: (i, 0))]
        args += [rope_q[0], rope_q[1]]
    in_specs += [
        pl.BlockSpec((None, KV_INNER, n_keys), lambda b, i: (b, 0, 0)),
        pl.BlockSpec((None, KV_HEADS, n_keys, 2 * HEAD_DIM), lambda b, i: (b, 0, 0, 0)),
    ]
    args += [kt, v4]
    return pl.pallas_call(
        functools.partial(_attn_kernel, rope=rope, tq=tq),
        grid=(n_seq, nq),
        in_specs=in_specs,
        out_specs=pl.BlockSpec((tq, ATTN_INNER), lambda b, i: (b * nq + i, 0)),
        out_shape=jax.ShapeDtypeStruct((t, ATTN_INNER), BF16),
        compiler_params=_cparams("parallel", "arbitrary"),
        name="attention",
    )(*args)


def _pool_kernel(p_ref, w_ref, scale_ref, o_ref, *, seq_len):
    n_chunks = seq_len // CHUNK
    win = CHUNK + 2 * HALO
    t_loc = lax.broadcasted_iota(I32, (CHUNK, win), 0)
    r_loc = lax.broadcasted_iota(I32, (CHUNK, win), 1) - HALO
    t_col = lax.broadcasted_iota(I32, (CHUNK, 1), 0)

    def chunk(c, carry):
        s = pl.multiple_of(c * CHUNK, CHUNK)
        w = _chunk_window(p_ref, c, n_chunks, seq_len).astype(BF16)
        outs = []
        for g, width in enumerate(POOL_WINDOWS):
            half = width // 2
            d = r_loc - t_loc
            band = jnp.where((d >= -half) & (d < width - half), 1.0, 0.0).astype(BF16)
            wg = w[:, g * POOL_GROUP_DIM:(g + 1) * POOL_GROUP_DIM]
            sums = jnp.dot(band, wg, preferred_element_type=F32)
            tg = t_col + s
            cnt = (jnp.minimum(tg + (width - half), seq_len) - jnp.maximum(tg - half, 0)).astype(F32)
            cur = wg[HALO:HALO + CHUNK, :].astype(F32)
            diff = sums / cnt - cur
            outs.append(jnp.dot(diff.astype(BF16), w_ref[g], preferred_element_type=F32))
        o_ref[pl.ds(s, CHUNK), :] = (jnp.concatenate(outs, axis=1) * scale_ref[...]).astype(o_ref.dtype)
        return carry

    lax.fori_loop(0, n_chunks, chunk, 0)


def pool_mixer(main, pool_w, layer, pool_scale, n_seq, seq_len):
    t = n_seq * seq_len
    return pl.pallas_call(
        functools.partial(_pool_kernel, seq_len=seq_len),
        grid=(n_seq,),
        in_specs=[
            pl.BlockSpec((seq_len, POOL_INNER), lambda b: (b, (3 * D_MODEL + SSD_INNER + ATTN_INNER) // POOL_INNER)),
            pl.BlockSpec((None, POOL_GROUPS, POOL_GROUP_DIM, POOL_GROUP_DIM), lambda b: (layer, 0, 0, 0)),
            pl.BlockSpec((1, POOL_INNER), lambda b: (0, 0)),
        ],
        out_specs=pl.BlockSpec((seq_len, POOL_INNER), lambda b: (b, 0)),
        out_shape=jax.ShapeDtypeStruct((t, POOL_INNER), BF16),
        compiler_params=_cparams("parallel"),
        name="pool_mixer",
    )(main, pool_w, pool_scale)


def _merge_kernel(ys_ref, ya_ref, yp_ref, g0_ref, g1_ref, g2_ref, ws_ref, wa_ref, wp_ref, o_ref):
    acc = jax.nn.sigmoid(g0_ref[...].astype(F32)) * jnp.dot(ys_ref[...], ws_ref[...], preferred_element_type=F32)
    acc = acc + jax.nn.sigmoid(g1_ref[...].astype(F32)) * jnp.dot(ya_ref[...], wa_ref[...],
                                                                  preferred_element_type=F32)
    acc = acc + jax.nn.sigmoid(g2_ref[...].astype(F32)) * jnp.dot(yp_ref[...], wp_ref[...],
                                                                  preferred_element_type=F32)
    o_ref[...] = acc.astype(o_ref.dtype)


def merge_branches(y_ssd, o_attn, o_pool, main, lw, tm=512):
    t = y_ssd.shape[0]
    act = lambda: pl.BlockSpec((tm, SSD_INNER), lambda i: (i, 0))
    gate = lambda k: pl.BlockSpec((tm, D_MODEL), lambda i: (i, k))
    wspec = lambda: pl.BlockSpec((None, SSD_INNER, D_MODEL), lambda i: (lw["layer"], 0, 0))
    return pl.pallas_call(
        _merge_kernel,
        grid=(t // tm,),
        in_specs=[act(), act(), act(), gate(0), gate(1), gate(2), wspec(), wspec(), wspec()],
        out_specs=pl.BlockSpec((tm, D_MODEL), lambda i: (i, 0)),
        out_shape=jax.ShapeDtypeStruct((t, D_MODEL), BF16),
        compiler_params=_cparams("parallel"),
        name="merge_branches",
    )(y_ssd, o_attn, o_pool, main, main, main, lw["w_ssd_out"], lw["w_attn_out"], lw["w_pool_out"])


def _route(logits_t, bias_col):
    scores = jax.nn.sigmoid(logits_t)
    biased = scores + bias_col
    sc = [scores[e:e + 1, :] for e in range(N_EXPERTS)]
    bi = [biased[e:e + 1, :] for e in range(N_EXPERTS)]
    group_scores = []
    for g in range(N_EXPERT_GROUPS):
        a, b, c, d = bi[4 * g:4 * g + 4]
        hi1, lo1 = jnp.maximum(a, b), jnp.minimum(a, b)
        hi2, lo2 = jnp.maximum(c, d), jnp.minimum(c, d)
        top1 = jnp.maximum(hi1, hi2)
        top2 = jnp.maximum(jnp.minimum(hi1, hi2), jnp.maximum(lo1, lo2))
        group_scores.append(top1 + top2)
    best = jnp.zeros_like(group_scores[0], dtype=I32)
    best_v = group_scores[0]
    for g in range(1, N_EXPERT_GROUPS):
        upd = group_scores[g] > best_v
        best = jnp.where(upd, g, best)
        best_v = jnp.where(upd, group_scores[g], best_v)
    vb, vs = [], []
    for j in range(EXPERTS_PER_GROUP):
        b_j, s_j = bi[j], sc[j]
        for g in range(1, N_EXPERT_GROUPS):
            sel = best == g
            b_j = jnp.where(sel, bi[4 * g + j], b_j)
            s_j = jnp.where(sel, sc[4 * g + j], s_j)
        vb.append(b_j)
        vs.append(s_j)

    def first_argmax(vals, excluded=None):
        idx = None
        for j in range(EXPERTS_PER_GROUP):
            v = vals[j] if excluded is None else jnp.where(excluded == j, -jnp.inf, vals[j])
            if idx is None:
                idx, cur = jnp.zeros_like(best), v
            else:
                upd = v > cur
                idx = jnp.where(upd, j, idx)
                cur = jnp.where(upd, v, cur)
        return idx

    i0 = first_argmax(vb)
    i1 = first_argmax(vb, excluded=i0)

    def pick(vals, idx):
        out = vals[0]
        for j in range(1, EXPERTS_PER_GROUP):
            out = jnp.where(idx == j, vals[j], out)
        return out

    s0, s1 = pick(vs, i0), pick(vs, i1)
    tot = s0 + s1
    ids = jnp.concatenate([best * EXPERTS_PER_GROUP + i0, best * EXPERTS_PER_GROUP + i1], axis=0)
    wts = jnp.concatenate([s0 / tot, s1 / tot], axis=0)
    return ids, wts


def _outproj_kernel(x_ref, m_ref, wo_ref, g1_ref, gain_ref, sh_ref, sc_ref, wr_ref, wr_hi_ref, br_ref,
                    xo_ref, h_ref, ids_ref, rank_ref, wcol_ref, cnt_ref, run_ref):
    tm = x_ref.shape[0]

    @pl.when(pl.program_id(0) == 0)
    def _():
        run_ref[...] = jnp.zeros_like(run_ref)

    mix = jnp.dot(m_ref[...], wo_ref[...], preferred_element_type=F32)
    xn = x_ref[...] + g1_ref[...] * mix
    xo_ref[...] = xn
    h = _rms(xn) * gain_ref[...] * (1.0 + sc_ref[...]) + sh_ref[...]
    h_hi, h_lo = _split_bf16(h)
    for j in range(ROW_TILES):
        h_ref[pl.ds(j, tm, stride=ROW_TILES), :] = h[:, j * LANES:(j + 1) * LANES]
    lg = jnp.dot(h_hi, wr_ref[...], preferred_element_type=F32) + jnp.dot(h_lo, wr_hi_ref[...],
                                                                         preferred_element_type=F32)
    lg_t = lg.T
    logits_t = lg_t[0:N_EXPERTS, :] + lg_t[N_EXPERTS:2 * N_EXPERTS, :]
    ids, wts = _route(logits_t, br_ref[...])

    def fold(rows):
        return jnp.concatenate([rows[k:k + 1, j * LANES:(j + 1) * LANES] for k in range(2)
                                for j in range(tm // LANES)], axis=0)

    ids_ref[...] = fold(ids)
    e_iota = lax.broadcasted_iota(I32, (N_EXPERTS, tm), 0)
    earlier = (lax.broadcasted_iota(I32, (tm, tm), 0) < lax.broadcasted_iota(I32, (tm, tm), 1))
    earlier = jnp.where(earlier, 1.0, 0.0).astype(BF16)
    oh0 = jnp.where(e_iota == ids[0:1, :], 1.0, 0.0)
    oh1 = jnp.where(e_iota == ids[1:2, :], 1.0, 0.0)
    before0 = jnp.dot(oh0.astype(BF16), earlier, preferred_element_type=F32)
    before1 = jnp.dot(oh1.astype(BF16), earlier, preferred_element_type=F32)
    c0 = jnp.sum(oh0, axis=1, keepdims=True)
    c1 = jnp.sum(oh1, axis=1, keepdims=True)
    running = run_ref[...]
    rank0 = jnp.sum(oh0 * (before0 + running), axis=0, keepdims=True)
    rank1 = jnp.sum(oh1 * (before1 + running + c0), axis=0, keepdims=True)
    rank_ref[...] = fold(jnp.concatenate([rank0, rank1], axis=0).astype(I32))
    running = running + c0 + c1
    run_ref[...] = running
    cnt_ref[...] = jnp.broadcast_to(running, cnt_ref.shape)
    r_iota = lax.broadcasted_iota(I32, (LANES, tm), 0)
    w_rows = jnp.where(r_iota == 0, wts[0:1, :], jnp.where(r_iota == 1, wts[1:2, :], 0.0))
    wcol_ref[...] = w_rows.T


def outproj_route(x, merged, mods, lw, gw, path):
    t = x.shape[0]
    tm = ROUTE_TILE
    rf = _row_fn(path, tm)
    row = lambda: pl.BlockSpec((1, D_MODEL), lambda i: (0, 0))
    return pl.pallas_call(
        _outproj_kernel,
        grid=(t // tm,),
        in_specs=[
            pl.BlockSpec((tm, D_MODEL), lambda i: (i, 0)),
            pl.BlockSpec((tm, D_MODEL), lambda i: (i, 0)),
            pl.BlockSpec((None, D_MODEL, D_MODEL), lambda i: (lw["layer"], 0, 0)),
            _mod_spec(2, rf),
            row(),
            _mod_spec(3, rf),
            _mod_spec(4, rf),
            pl.BlockSpec((D_MODEL, LANES), lambda i: (0, 0)),
            pl.BlockSpec((D_MODEL, LANES), lambda i: (0, 0)),
            pl.BlockSpec((N_EXPERTS, 1), lambda i: (0, 0)),
        ],
        out_specs=[
            pl.BlockSpec((tm, D_MODEL), lambda i: (i, 0)),
            pl.BlockSpec((tm * ROW_TILES, LANES), lambda i: (i, 0)),
            pl.BlockSpec((None, 2 * tm // LANES, LANES), lambda i: (i, 0, 0)),
            pl.BlockSpec((None, 2 * tm // LANES, LANES), lambda i: (i, 0, 0)),
            pl.BlockSpec((tm, LANES), lambda i: (i, 0)),
            pl.BlockSpec((N_EXPERTS, LANES), lambda i: (0, 0)),
        ],
        out_shape=[
            jax.ShapeDtypeStruct((t, D_MODEL), F32),
            jax.ShapeDtypeStruct((t * ROW_TILES, LANES), F32),
            jax.ShapeDtypeStruct((t // tm, 2 * tm // LANES, LANES), I32),
            jax.ShapeDtypeStruct((t // tm, 2 * tm // LANES, LANES), I32),
            jax.ShapeDtypeStruct((t, LANES), F32),
            jax.ShapeDtypeStruct((N_EXPERTS, LANES), F32),
        ],
        scratch_shapes=[pltpu.VMEM((N_EXPERTS, 1), F32)],
        compiler_params=_cparams("arbitrary"),
        name="outproj_route",
    )(x, merged, lw["w_out"], mods, lw["norm_ffn"], mods, mods, gw["w_router_cat"], gw["w_router_hi"],
      gw["b_router"])


def _moe_kernel(tile_expert_ref, src_ref, n_used_ref, h_hbm, wg_ref, wu_ref, wd_ref, o_ref, xbuf0, xbuf1, sem):
    i = pl.program_id(0)
    n_used = n_used_ref[0]
    bufs = (xbuf0, xbuf1)

    def start_tile(tile, slot):
        for r in range(MOE_TILE):
            row = pl.multiple_of(src_ref[tile * MOE_TILE + r] * ROW_TILES, ROW_TILES)
            pltpu.make_async_copy(h_hbm.at[pl.ds(row, ROW_TILES)], bufs[slot].at[pl.ds(r * ROW_TILES, ROW_TILES)],
                                  sem.at[slot]).start()

    def wait_tile(slot):
        pltpu.make_async_copy(h_hbm.at[pl.ds(0, MOE_TILE * ROW_TILES)], bufs[slot], sem.at[slot]).wait()

    def compute(slot):
        x = jnp.concatenate([bufs[slot][pl.ds(j, MOE_TILE, stride=ROW_TILES), :].astype(BF16)
                             for j in range(ROW_TILES)], axis=1)
        a = jnp.dot(x, wg_ref[...], preferred_element_type=F32)
        u = jnp.dot(x, wu_ref[...], preferred_element_type=F32)
        act = _silu(a) * u
        o_ref[...] = jnp.dot(act.astype(BF16), wd_ref[...], preferred_element_type=F32).astype(o_ref.dtype)

    @pl.when(i == 0)
    def _():
        start_tile(0, 0)

    for slot in range(2):
        @pl.when((i % 2 == slot) & (i + 1 < n_used))
        def _(slot=slot):
            wait_tile(slot)
            start_tile(i + 1, 1 - slot)
            compute(slot)

        @pl.when((i % 2 == slot) & (i + 1 == n_used))
        def _(slot=slot):
            wait_tile(slot)
            compute(slot)

    @pl.when(i >= n_used)
    def _():
        o_ref[...] = jnp.zeros_like(o_ref)


def moe_experts(h, src, tile_expert, n_used, lw):
    rows = src.shape[0]
    n_tiles = rows // MOE_TILE
    layer = lw["layer"]
    grid_spec = pltpu.PrefetchScalarGridSpec(
        num_scalar_prefetch=3,
        grid=(n_tiles,),
        in_specs=[
            pl.BlockSpec(memory_space=pl.ANY),
            pl.BlockSpec((None, None, D_MODEL, D_FF), lambda i, te, sr, nu: (layer, te[i], 0, 0)),
            pl.BlockSpec((None, None, D_MODEL, D_FF), lambda i, te, sr, nu: (layer, te[i], 0, 0)),
            pl.BlockSpec((None, None, D_FF, D_MODEL), lambda i, te, sr, nu: (layer, te[i], 0, 0)),
        ],
        out_specs=pl.BlockSpec((MOE_TILE, D_MODEL), lambda i, te, sr, nu: (i, 0)),
        scratch_shapes=[pltpu.VMEM((MOE_TILE * ROW_TILES, LANES), F32), pltpu.VMEM((MOE_TILE * ROW_TILES, LANES), F32),
                        pltpu.SemaphoreType.DMA((2,))],
    )
    return pl.pallas_call(
        _moe_kernel,
        grid_spec=grid_spec,
        out_shape=jax.ShapeDtypeStruct((rows, D_MODEL), BF16),
        compiler_params=_cparams("arbitrary"),
        name="moe_experts",
    )(tile_expert, src, n_used, h, lw["w_gate_ff"], lw["w_up_ff"], lw["w_down_ff"])


def _moe_plan(ids, rank, counts):
    two_t = ids.size
    ids = ids.reshape(two_t)
    rank = rank.reshape(two_t)
    counts = counts[:, 0].astype(I32)
    padded = ((counts + MOE_TILE - 1) // MOE_TILE) * MOE_TILE
    ends = jnp.cumsum(padded)
    starts = ends - padded
    onehot = ids[:, None] == jnp.arange(N_EXPERTS, dtype=I32)[None, :]
    pos = jnp.sum(jnp.where(onehot, starts[None, :], 0), axis=1) + rank
    n_rows = two_t + N_EXPERTS * MOE_TILE
    n_tiles = n_rows // MOE_TILE
    pair = jnp.arange(two_t, dtype=I32)
    token = (pair // (2 * ROUTE_TILE)) * ROUTE_TILE + pair % ROUTE_TILE
    src = jnp.zeros((n_rows,), I32).at[pos].set(token, mode="promise_in_bounds", unique_indices=True)
    tile_start = jnp.arange(n_tiles, dtype=I32) * MOE_TILE
    tile_expert = jnp.minimum(jnp.sum((tile_start[:, None] >= ends[None, :]).astype(I32), axis=1), N_EXPERTS - 1)
    n_used = (ends[-1] // MOE_TILE).reshape(1).astype(I32)
    return pos, src, tile_expert.astype(I32), n_used


def moe_ffn(h2, ids, rank, counts, lw):
    pos, src, tile_expert, n_used = _moe_plan(ids, rank, counts)
    ys = moe_experts(h2, src, tile_expert, n_used, lw)
    return ys.at[pos].get(mode="promise_in_bounds", unique_indices=True)


def _rope_tables(n_tokens):
    rows = n_tokens // GRID_W
    row = jnp.repeat(jnp.arange(rows, dtype=F32), GRID_W)
    col = jnp.tile(jnp.arange(GRID_W, dtype=F32), rows)
    n_freq = HEAD_DIM // 4
    inv = ROPE_THETA ** (-jnp.arange(n_freq, dtype=F32) / n_freq)
    ang = jnp.concatenate([row[:, None] * inv, col[:, None] * inv], axis=-1)
    cos = jnp.repeat(jnp.cos(ang), 2, axis=-1)
    sin = jnp.repeat(jnp.sin(ang), 2, axis=-1)
    sign = jnp.tile(jnp.array([-1.0, 1.0], F32), HEAD_DIM // 2)
    sin = sin * sign
    return cos, sin


def _prep_stacked_weights(p):
    w_in = p["w_in"]
    depth = w_in.shape[0]
    w_main = jnp.concatenate([w_in[:, :, _O_GATE:], w_in[:, :, _O_Z:_O_XBC], w_in[:, :, _O_Q:_O_K],
                              w_in[:, :, _O_PIN:_O_GATE], w_in[:, :, _O_XBC:_O_DTF]], axis=2).astype(BF16)
    w_small = jnp.concatenate([w_in[:, :, _O_K:_O_PIN], w_in[:, :, _O_DTF:_O_Q],
                               jnp.zeros((depth, D_MODEL, LANES - 2 * SSD_HEADS), F32)], axis=2).astype(BF16)
    out = {"w_main": w_main, "w_small": w_small}
    for name in ("pool_w", "w_ssd_out", "w_attn_out", "w_pool_out", "w_out", "w_gate_ff", "w_up_ff", "w_down_ff"):
        out[name] = p[name].astype(BF16)
    return out


def _prep_layer_weights(l, p, stacked):
    pad_lanes = lambda v: jnp.concatenate([v.reshape(-1), jnp.zeros((LANES - v.size,), F32)]).reshape(1, LANES)
    return {
        **stacked,
        "layer": l,
        "conv_w": jnp.concatenate([p["conv_w"][l], jnp.zeros((8 - CONV_W, XBC_DIM), F32)], axis=0),
        "conv_b": p["conv_b"][l].reshape(1, XBC_DIM),
        "dt_bias": pad_lanes(p["dt_bias"][l]),
        "a_log": pad_lanes(p["a_log"][l]),
        "d_skip": jnp.repeat(p["d_skip"][l], SSD_HEADDIM).reshape(1, SSD_INNER),
        "ssd_norm": p["ssd_norm"][l].reshape(1, SSD_INNER),
        "q_gain": jnp.tile(p["q_norm"][l], LANES // HEAD_DIM).reshape(1, LANES),
        "k_gain": jnp.tile(p["k_norm"][l], KV_HEADS).reshape(1, KV_INNER),
        "pool_scale": p["pool_scale"][l].reshape(1, POOL_INNER),
        "norm_mix": p["norm_mix"][l].reshape(1, D_MODEL),
        "norm_ffn": p["norm_ffn"][l].reshape(1, D_MODEL),
    }


def _mixer_stage(h, x, mods, lw, gw, path, n_seq, seq_len, tq, h0, rope, cache):
    t = n_seq * seq_len
    main = matmul(h, lw["w_main"], lw["layer"], BF16, min(t, 1024), 1536, "in_proj_main")
    small = matmul(h, lw["w_small"], lw["layer"], F32, min(t, 1024), SMALL_N, "in_proj_small")
    y_ssd, hf, hb = ssd_mixer(main, small, h0[0], h0[1], lw, n_seq, seq_len)
    if rope is None:
        k_norm, kt, v4 = kv_prep(small, lw["k_gain"], n_seq, seq_len)
        o_attn = attention(main, lw["q_gain"], kt, v4, n_seq, seq_len, tq)
    else:
        k_norm = None
        kt, v4 = kv_prep(small, lw["k_gain"], n_seq, seq_len, rope_kv=rope[0], cache=cache)
        o_attn = attention(main, lw["q_gain"], kt, v4, n_seq, seq_len, tq, rope_q=rope[1])
    o_pool = pool_mixer(main, lw["pool_w"], lw["layer"], lw["pool_scale"], n_seq, seq_len)
    merged = merge_branches(y_ssd, o_attn, o_pool, main, lw)
    x_new, h2, ids, rank, wcol, counts = outproj_route(x, merged, mods, lw, gw, path)
    y2 = moe_ffn(h2, ids, rank, counts, lw)
    return x_new, y2, wcol, k_norm, small, hf, hb


def kernel(x_prompt, x_sample, cache_k, cache_v, state_ssd_fwd, state_ssd_bwd, c, c_ctx, w_ada, b_ada, norm_mix,
           norm_ffn, w_in, conv_w, conv_b, a_log, dt_bias, d_skip, ssd_norm, q_norm, k_norm, pool_w, pool_scale,
           w_ssd_out, w_attn_out, w_pool_out, w_out, w_router, b_router, w_gate_ff, w_up_ff, w_down_ff, norm_final):
    p = dict(norm_mix=norm_mix, norm_ffn=norm_ffn, w_in=w_in, conv_w=conv_w, conv_b=conv_b, a_log=a_log,
             dt_bias=dt_bias, d_skip=d_skip, ssd_norm=ssd_norm, q_norm=q_norm, k_norm=k_norm, pool_w=pool_w,
             pool_scale=pool_scale, w_ssd_out=w_ssd_out, w_attn_out=w_attn_out, w_pool_out=w_pool_out, w_out=w_out,
             w_gate_ff=w_gate_ff, w_up_ff=w_up_ff, w_down_ff=w_down_ff)
    bp, lc, _ = x_prompt.shape
    depth = w_in.shape[0]
    bs, ls, _ = x_sample.shape
    past = cache_k.shape[2]
    tc, tl = bp * lc, bs * ls

    mod_rows = -(-(1 + bs) // 8) * 8
    cond = jnp.concatenate([c_ctx[None, :], c, jnp.zeros((mod_rows - 1 - bs, D_MODEL), F32)], axis=0)
    mods_all = ada_mods(cond, w_ada, b_ada).reshape(depth, mod_rows, 6, 1, D_MODEL)

    wr_hi, wr_lo = _split_bf16(w_router)
    zpad = jnp.zeros((D_MODEL, LANES - 2 * N_EXPERTS), BF16)
    gw = {
        "w_router_cat": jnp.concatenate([wr_hi, wr_lo, zpad], axis=1),
        "w_router_hi": jnp.concatenate([wr_hi, jnp.zeros((D_MODEL, LANES - N_EXPERTS), BF16)], axis=1),
        "b_router": b_router.reshape(N_EXPERTS, 1),
    }
    cos, sin = _rope_tables(ls)
    rope = ((jnp.tile(cos, (1, KV_HEADS)), jnp.tile(sin, (1, KV_HEADS))),
            (jnp.tile(cos, (1, LANES // HEAD_DIM)), jnp.tile(sin, (1, LANES // HEAD_DIM))))
    cache_k4 = cache_k.reshape(bs, depth, past, KV_INNER)
    cache_v4 = cache_v.reshape(bs, depth, past, KV_INNER)
    pairs = SSD_HEADS // 2

    def pack_state(st):
        st = st.reshape(bs, depth, pairs, 2, SSD_HEADDIM, D_STATE)
        return st.transpose(0, 1, 2, 5, 3, 4).reshape(bs, depth, pairs, D_STATE, 2 * SSD_HEADDIM)

    def unpack_state(st):
        return st.reshape(st.shape[0], SSD_HEADS, SSD_HEADDIM, D_STATE)

    st_f = pack_state(state_ssd_fwd)
    st_b = pack_state(state_ssd_bwd)

    path_c = (0, False, lc)
    path_l = (1, True, ls)
    xc = x_prompt.reshape(tc, D_MODEL)
    xl = x_sample.reshape(tl, D_MODEL)
    ks, vs, hfs, hbs = [], [], [], []
    hc = hl = None
    y2c = y2l = wc = wl = None
    stacked = _prep_stacked_weights(p)
    for l in range(depth):
        lw = _prep_layer_weights(l, p, stacked)
        mods = mods_all[l]
        if l == 0:
            hc = modulate_first(xc, mods, lw["norm_mix"], path_c)
            hl = modulate_first(xl, mods, lw["norm_mix"], path_l)
        else:
            xc, hc = residual_modulate(xc, y2c, wc, mods_all[l - 1], mods, lw["norm_mix"], path_c)
            xl, hl = residual_modulate(xl, y2l, wl, mods_all[l - 1], mods, lw["norm_mix"], path_l)
        xc, y2c, wc, k_c, small_c, hf, hb = _mixer_stage(
            hc, xc, mods, lw, gw, path_c, bp, lc, lc, (None, None), None, None)
        ks.append(k_c.reshape(bp, lc, KV_HEADS, HEAD_DIM))
        vs.append(small_c[:, KV_INNER:2 * KV_INNER].reshape(bp, lc, KV_HEADS, HEAD_DIM))
        hfs.append(unpack_state(hf))
        hbs.append(unpack_state(hb))
        xl, y2l, wl, _, _, _, _ = _mixer_stage(
            hl, xl, mods, lw, gw, path_l, bs, ls, CHUNK, (st_f[:, l], st_b[:, l]), rope, (cache_k4, cache_v4, l))
    gain_f = norm_final.reshape(1, D_MODEL)
    y_prompt = residual_final(xc, y2c, wc, mods_all[depth - 1], gain_f, path_c).reshape(bp, lc, D_MODEL)
    y_sample = residual_final(xl, y2l, wl, mods_all[depth - 1], gain_f, path_l).reshape(bs, ls, D_MODEL)
    return (y_prompt, y_sample, jnp.stack(ks, axis=1), jnp.stack(vs, axis=1), jnp.stack(hfs, axis=1),
            jnp.stack(hbs, axis=1))
```

```python
import functools

import jax
import jax.numpy as jnp
from jax import lax
from jax.experimental import pallas as pl
from jax.experimental.pallas import tpu as pltpu

F32 = jnp.float32
BF16 = jnp.bfloat16
I32 = jnp.int32

D_MODEL = 2048
GRID_W = 64
EPS = 1e-6

SSD_HEADS = 16
SSD_HEADDIM = 64
SSD_INNER = SSD_HEADS * SSD_HEADDIM
SSD_GROUPS = 4
D_STATE = 64
CONV_W = 5
CHUNK = 128
XBC_DIM = SSD_INNER + 2 * SSD_GROUPS * D_STATE

N_HEADS = 16
KV_HEADS = 4
HEAD_DIM = 64
ATTN_INNER = N_HEADS * HEAD_DIM
KV_INNER = KV_HEADS * HEAD_DIM
ROPE_THETA = 10000.0

POOL_GROUPS = 4
POOL_INNER = 1024
POOL_GROUP_DIM = POOL_INNER // POOL_GROUPS
POOL_WINDOWS = (2, 4, 8, 16)

N_EXPERTS = 16
N_EXPERT_GROUPS = 4
EXPERTS_PER_GROUP = N_EXPERTS // N_EXPERT_GROUPS
D_FF = 512

_O_Z = 0
_O_XBC = _O_Z + SSD_INNER
_O_DTF = _O_XBC + XBC_DIM
_O_DTB = _O_DTF + SSD_HEADS
_O_Q = _O_DTB + SSD_HEADS
_O_K = _O_Q + ATTN_INNER
_O_V = _O_K + KV_INNER
_O_PIN = _O_V + KV_INNER
_O_GATE = _O_PIN + POOL_INNER

M_Z = 0
M_XBC = M_Z + SSD_INNER
M_Q = M_XBC + XBC_DIM
M_K = M_Q + ATTN_INNER
M_V = M_K + KV_INNER
M_PIN = M_V + KV_INNER
M_GATE = M_PIN + POOL_INNER
MAIN_N = M_GATE + 3 * D_MODEL
MAIN_TN = 1024
LANES = 128

HALO = 16
MOE_TILE = 256
ROUTE_TILE = 512
ROW_TILES = D_MODEL // LANES
VMEM_LIMIT = 56 * 1024 * 1024


def _cparams(*sem):
    return pltpu.CompilerParams(dimension_semantics=sem, vmem_limit_bytes=VMEM_LIMIT)


def _silu(x):
    return x * jax.nn.sigmoid(x)


def _rms(xf):
    return xf * lax.rsqrt(jnp.mean(xf * xf, axis=-1, keepdims=True) + EPS)


def _split_bf16(x):
    hi = x.astype(BF16)
    lo = (x - hi.astype(F32)).astype(BF16)
    return hi, lo


def _ada_kernel(cond_ref, w_ref, b_ref, o_ref):
    c = _silu(cond_ref[...])
    o_ref[...] = jnp.dot(c.astype(BF16), w_ref[...].astype(BF16), preferred_element_type=F32) + b_ref[...]


def ada_mods(cond, w_ada, b_ada, tn=1024):
    rows = cond.shape[0]
    depth, _, n = w_ada.shape
    return pl.pallas_call(
        _ada_kernel,
        grid=(depth, n // tn),
        in_specs=[
            pl.BlockSpec((rows, D_MODEL), lambda l, j: (0, 0)),
            pl.BlockSpec((None, D_MODEL, tn), lambda l, j: (l, 0, j)),
            pl.BlockSpec((None, 1, tn), lambda l, j: (l, 0, j)),
        ],
        out_specs=pl.BlockSpec((None, rows, tn), lambda l, j: (l, 0, j)),
        out_shape=jax.ShapeDtypeStruct((depth, rows, n), F32),
        compiler_params=_cparams("parallel", "parallel"),
        name="ada_mods",
    )(cond, w_ada, b_ada.reshape(depth, 1, n))


def _mod_spec(which, row_of_tile):
    return pl.BlockSpec((None, None, 1, D_MODEL), lambda i: (row_of_tile(i), which, 0, 0))


def _row_fn(path, tm):
    row0, per_seq, seq_len = path
    if per_seq:
        return lambda i: row0 + (i * tm) // seq_len
    return lambda i: row0


def _modulate_kernel(x_ref, g_ref, sh_ref, sc_ref, h_ref):
    xf = x_ref[...]
    h_ref[...] = (_rms(xf) * g_ref[...] * (1.0 + sc_ref[...]) + sh_ref[...]).astype(h_ref.dtype)


def modulate_first(x, mods, gain, path, tm=512):
    t = x.shape[0]
    rf = _row_fn(path, tm)
    return pl.pallas_call(
        _modulate_kernel,
        grid=(t // tm,),
        in_specs=[
            pl.BlockSpec((tm, D_MODEL), lambda i: (i, 0)),
            pl.BlockSpec((1, D_MODEL), lambda i: (0, 0)),
            _mod_spec(0, rf),
            _mod_spec(1, rf),
        ],
        out_specs=pl.BlockSpec((tm, D_MODEL), lambda i: (i, 0)),
        out_shape=jax.ShapeDtypeStruct((t, D_MODEL), BF16),
        compiler_params=_cparams("parallel"),
        name="modulate_first",
    )(x, gain, mods, mods)


def _moe_residual(x_ref, ya_ref, yb_ref, w_ref, g2_ref):
    w = w_ref[...]
    y = w[:, 0:1] * ya_ref[...].astype(F32) + w[:, 1:2] * yb_ref[...].astype(F32)
    return x_ref[...] + g2_ref[...] * y


def _residual_modulate_kernel(x_ref, ya_ref, yb_ref, w_ref, g2_ref, gain_ref, sh_ref, sc_ref, xo_ref, h_ref):
    xn = _moe_residual(x_ref, ya_ref, yb_ref, w_ref, g2_ref)
    xo_ref[...] = xn
    h_ref[...] = (_rms(xn) * gain_ref[...] * (1.0 + sc_ref[...]) + sh_ref[...]).astype(h_ref.dtype)


def residual_modulate(x, y2, wcol, mods_prev, mods_next, gain_next, path):
    t = x.shape[0]
    tm = ROUTE_TILE
    nt = t // tm
    rf = _row_fn(path, tm)
    return pl.pallas_call(
        _residual_modulate_kernel,
        grid=(nt,),
        in_specs=[
            pl.BlockSpec((tm, D_MODEL), lambda i: (i, 0)),
            pl.BlockSpec((tm, D_MODEL), lambda i: (2 * i, 0)),
            pl.BlockSpec((tm, D_MODEL), lambda i: (2 * i + 1, 0)),
            pl.BlockSpec((tm, LANES), lambda i: (i, 0)),
            _mod_spec(5, rf),
            pl.BlockSpec((1, D_MODEL), lambda i: (0, 0)),
            _mod_spec(0, rf),
            _mod_spec(1, rf),
        ],
        out_specs=[
            pl.BlockSpec((tm, D_MODEL), lambda i: (i, 0)),
            pl.BlockSpec((tm, D_MODEL), lambda i: (i, 0)),
        ],
        out_shape=[
            jax.ShapeDtypeStruct((t, D_MODEL), F32),
            jax.ShapeDtypeStruct((t, D_MODEL), BF16),
        ],
        compiler_params=_cparams("parallel"),
        name="residual_modulate",
    )(x, y2, y2, wcol, mods_prev, gain_next, mods_next, mods_next)


def _residual_final_kernel(x_ref, ya_ref, yb_ref, w_ref, g2_ref, gain_ref, y_ref):
    xn = _moe_residual(x_ref, ya_ref, yb_ref, w_ref, g2_ref)
    y_ref[...] = _rms(xn) * gain_ref[...]


def residual_final(x, y2, wcol, mods_prev, gain, path):
    t = x.shape[0]
    tm = ROUTE_TILE
    nt = t // tm
    rf = _row_fn(path, tm)
    return pl.pallas_call(
        _residual_final_kernel,
        grid=(nt,),
        in_specs=[
            pl.BlockSpec((tm, D_MODEL), lambda i: (i, 0)),
            pl.BlockSpec((tm, D_MODEL), lambda i: (2 * i, 0)),
            pl.BlockSpec((tm, D_MODEL), lambda i: (2 * i + 1, 0)),
            pl.BlockSpec((tm, LANES), lambda i: (i, 0)),
            _mod_spec(5, rf),
            pl.BlockSpec((1, D_MODEL), lambda i: (0, 0)),
        ],
        out_specs=pl.BlockSpec((tm, D_MODEL), lambda i: (i, 0)),
        out_shape=jax.ShapeDtypeStruct((t, D_MODEL), F32),
        compiler_params=_cparams("parallel"),
        name="residual_final",
    )(x, y2, y2, wcol, mods_prev, gain)


def _mm_kernel(a_ref, b_ref, o_ref):
    o_ref[...] = jnp.dot(a_ref[...], b_ref[...], preferred_element_type=F32).astype(o_ref.dtype)


def matmul(a, b, layer, out_dtype, tm, tn, name):
    m, k = a.shape
    n = b.shape[2]
    return pl.pallas_call(
        _mm_kernel,
        grid=(m // tm, n // tn),
        in_specs=[
            pl.BlockSpec((tm, k), lambda i, j: (i, 0)),
            pl.BlockSpec((None, k, tn), lambda i, j: (layer, 0, j)),
        ],
        out_specs=pl.BlockSpec((tm, tn), lambda i, j: (i, j)),
        out_shape=jax.ShapeDtypeStruct((m, n), out_dtype),
        compiler_params=_cparams("parallel", "arbitrary"),
        name=name,
    )(a, b)


def _chunk_window(ref, c, n_chunks, seq_len):
    s = pl.multiple_of(c * CHUNK, CHUNK)
    cur = ref[pl.ds(s, CHUNK), :].astype(F32)
    sp = pl.multiple_of(jnp.maximum(s - HALO, 0), HALO)
    sn = pl.multiple_of(jnp.minimum(s + CHUNK, seq_len - HALO), HALO)
    prev = ref[pl.ds(sp, HALO), :].astype(F32)
    nxt = ref[pl.ds(sn, HALO), :].astype(F32)
    prev = jnp.where(c > 0, prev, 0.0)
    nxt = jnp.where(c < n_chunks - 1, nxt, 0.0)
    return jnp.concatenate([prev, cur, nxt], axis=0)


def _ssd_kernel(*refs, seq_len, has_h0):
    if has_h0:
        (zx_ref, dt_ref, h0f_ref, h0b_ref, cw_ref, cb_ref, dtb_ref, alog_ref, dskip_ref, gain_ref,
         expand_ref, colsel_ref, y_ref, hf_ref, hb_ref, xc_ref, sp_ref, yacc_ref, st_ref) = refs
    else:
        (zx_ref, dt_ref, cw_ref, cb_ref, dtb_ref, alog_ref, dskip_ref, gain_ref,
         expand_ref, colsel_ref, y_ref, hf_ref, hb_ref, xc_ref, sp_ref, yacc_ref, st_ref) = refs
        h0f_ref = h0b_ref = None
    z_ref = zx_ref.at[:, pl.ds(M_Z, SSD_INNER)]
    xbc_ref = zx_ref.at[:, pl.ds(M_XBC, XBC_DIM)]
    n_chunks = seq_len // CHUNK
    win = CHUNK + 2 * HALO

    def conv_chunk(c, carry):
        w = _chunk_window(xbc_ref, c, n_chunks, seq_len)
        acc = jnp.zeros((CHUNK, XBC_DIM), F32) + cb_ref[...]
        for j in range(CONV_W):
            shift = (CONV_W // 2 - j) % win
            wj = w if shift == 0 else pltpu.roll(w, shift, 0)
            acc = acc + cw_ref[j:j + 1, :] * wj[HALO:HALO + CHUNK, :]
        s = pl.multiple_of(c * CHUNK, CHUNK)
        xc_ref[pl.ds(s, CHUNK), :] = _silu(acc).astype(BF16)
        return carry

    lax.fori_loop(0, n_chunks, conv_chunk, 0)

    xdt = dt_ref[...] + dtb_ref[...]
    sp_ref[...] = jnp.maximum(xdt, 0.0) + jnp.log1p(jnp.exp(-jnp.abs(xdt)))
    a_row = -jnp.exp(alog_ref[...])

    rows = lax.broadcasted_iota(I32, (CHUNK, CHUNK), 0)
    cols = lax.broadcasted_iota(I32, (CHUNK, CHUNK), 1)
    lane_lo = cols < SSD_HEADDIM

    def cat2(x):
        hi, lo = _split_bf16(x)
        return jnp.concatenate([hi, lo], axis=1)

    def scan_chunk(c, forward):
        d = 0 if forward else 1
        off = d * SSD_HEADS
        s = pl.multiple_of(c * CHUNK, CHUNK)
        xcv = xc_ref[pl.ds(s, CHUNK), :]
        dtc = sp_ref[pl.ds(s, CHUNK), :]
        a = dtc * a_row
        tri = (rows >= cols) if forward else (cols >= rows)
        tri_b = jnp.where(tri, 1.0, 0.0).astype(BF16)
        a1 = a.astype(BF16)
        r1 = a - a1.astype(F32)
        a2 = r1.astype(BF16)
        a3 = (r1 - a2.astype(F32)).astype(BF16)
        cs = (jnp.dot(tri_b, a1, preferred_element_type=F32) + jnp.dot(tri_b, a2, preferred_element_type=F32)
              + jnp.dot(tri_b, a3, preferred_element_type=F32))
        cs_t = cs.T
        tot = cs[CHUNK - 1:CHUNK, :] if forward else cs[0:1, :]
        expand = expand_ref[d]
        dt_full = jnp.dot(cat2(dtc), expand, preferred_element_type=F32)
        dout_full = jnp.dot(cat2(jnp.exp(cs)), expand, preferred_element_type=F32)
        dst_full = jnp.dot(cat2(jnp.exp(tot - cs)), expand, preferred_element_type=F32)
        etot_full = dout_full[CHUNK - 1:CHUNK, :] if forward else dout_full[0:1, :]
        colb = jnp.dot(cat2(cs), colsel_ref[d], preferred_element_type=F32)
        xd = xcv[:, :SSD_INNER].astype(F32) * dt_full
        xdb = xd.astype(BF16)
        xdd = (xd * dst_full).astype(BF16)
        b_t = xcv[:, SSD_INNER:SSD_INNER + SSD_GROUPS * D_STATE].astype(F32).T.astype(BF16)
        zero = jnp.zeros((CHUNK, CHUNK), BF16)
        ys = []
        for g in range(SSD_GROUPS):
            bg_t = b_t[g * D_STATE:(g + 1) * D_STATE, :]
            cg = xcv[:, SSD_INNER + (SSD_GROUPS + g) * D_STATE:SSD_INNER + (SSD_GROUPS + g + 1) * D_STATE]
            gmat = jnp.dot(cg, bg_t, preferred_element_type=F32)
            for pp in range(2):
                j = 2 * g + pp
                ms = []
                for hh in range(2):
                    h = 2 * j + hh
                    diff = colb[:, h * CHUNK:(h + 1) * CHUNK] - cs_t[off + h:off + h + 1, :]
                    lm = jnp.exp(jnp.where(tri, diff, -jnp.inf))
                    ms.append((gmat * lm).astype(BF16))
                mcat = jnp.concatenate(ms, axis=1)
                xp = xdb[:, j * CHUNK:(j + 1) * CHUNK]
                xblk = jnp.concatenate([jnp.where(lane_lo, xp, zero), jnp.where(lane_lo, zero, xp)], axis=0)
                y_diag = jnp.dot(mcat, xblk, preferred_element_type=F32)
                st = st_ref[j]
                y_off = jnp.dot(cg, st.astype(BF16), preferred_element_type=F32) * dout_full[:, j * CHUNK:(j + 1) * CHUNK]
                contrib = jnp.dot(bg_t, xdd[:, j * CHUNK:(j + 1) * CHUNK], preferred_element_type=F32)
                st_ref[j] = st * etot_full[:, j * CHUNK:(j + 1) * CHUNK] + contrib
                ys.append(y_diag + y_off)
        return s, xcv, jnp.concatenate(ys, axis=1)

    def fwd_chunk(c, carry):
        s, _, y = scan_chunk(c, True)
        yacc_ref[pl.ds(s, CHUNK), :] = y
        return carry

    def bwd_chunk(i, carry):
        c = n_chunks - 1 - i
        s, xcv, y = scan_chunk(c, False)
        xs = xcv[:, :SSD_INNER].astype(F32)
        y = yacc_ref[pl.ds(s, CHUNK), :] + y + dskip_ref[...] * xs
        y = y * _silu(z_ref[pl.ds(s, CHUNK), :].astype(F32))
        y_ref[pl.ds(s, CHUNK), :] = (_rms(y) * gain_ref[...]).astype(y_ref.dtype)
        return carry

    def store_state(out_ref):
        for j in range(SSD_HEADS // 2):
            sq = jnp.concatenate([st_ref[j], jnp.zeros((2 * SSD_HEADDIM - D_STATE, 2 * SSD_HEADDIM), F32)], axis=0)
            out_ref[j] = sq.T[:, :D_STATE]

    if has_h0:
        st_ref[...] = h0f_ref[...]
    else:
        st_ref[...] = jnp.zeros_like(st_ref)
    lax.fori_loop(0, n_chunks, fwd_chunk, 0)
    store_state(hf_ref)
    if has_h0:
        st_ref[...] = h0b_ref[...]
    else:
        st_ref[...] = jnp.zeros_like(st_ref)
    lax.fori_loop(0, n_chunks, bwd_chunk, 0)
    store_state(hb_ref)


def ssd_mixer(main, small, h0f, h0b, lw, n_seq, seq_len):
    t = n_seq * seq_len
    has_h0 = h0f is not None
    pairs = SSD_HEADS // 2
    st_shape = (pairs, D_STATE, 2 * SSD_HEADDIM)
    out_st_shape = (pairs, 2 * SSD_HEADDIM, D_STATE)
    st_spec = pl.BlockSpec((None,) + st_shape, lambda b: (b, 0, 0, 0))
    out_st_spec = pl.BlockSpec((None,) + out_st_shape, lambda b: (b, 0, 0, 0))
    full = lambda shape: pl.BlockSpec(shape, lambda b: (0,) * len(shape))
    in_specs = [
        pl.BlockSpec((seq_len, M_Q), lambda b: (b, 0)),
        pl.BlockSpec((seq_len, LANES), lambda b: (b, 0)),
    ]
    args = [main, small]
    if has_h0:
        in_specs += [st_spec, st_spec]
        args += [h0f, h0b]
    k_head = (jnp.arange(2 * LANES, dtype=I32) % LANES)[None, :, None] - SSD_HEADS * jnp.arange(2, dtype=I32)[:, None, None]
    expand = (k_head == (jnp.arange(SSD_INNER, dtype=I32) // SSD_HEADDIM)[None, None, :]).astype(BF16)
    colsel = (k_head == (jnp.arange(SSD_HEADS * CHUNK, dtype=I32) // CHUNK)[None, None, :]).astype(BF16)
    in_specs += [full((8, XBC_DIM)), full((1, XBC_DIM)), full((1, LANES)), full((1, LANES)),
                 full((1, SSD_INNER)), full((1, SSD_INNER)), full((2, 2 * LANES, SSD_INNER)),
                 full((2, 2 * LANES, SSD_HEADS * CHUNK))]
    args += [lw["conv_w"], lw["conv_b"], lw["dt_bias"], lw["a_log"], lw["d_skip"], lw["ssd_norm"], expand, colsel]
    return pl.pallas_call(
        functools.partial(_ssd_kernel, seq_len=seq_len, has_h0=has_h0),
        grid=(n_seq,),
        in_specs=in_specs,
        out_specs=[pl.BlockSpec((seq_len, SSD_INNER), lambda b: (b, 0)), out_st_spec, out_st_spec],
        out_shape=[
            jax.ShapeDtypeStruct((t, SSD_INNER), BF16),
            jax.ShapeDtypeStruct((n_seq,) + out_st_shape, F32),
            jax.ShapeDtypeStruct((n_seq,) + out_st_shape, F32),
        ],
        scratch_shapes=[
            pltpu.VMEM((seq_len, XBC_DIM), BF16),
            pltpu.VMEM((seq_len, LANES), F32),
            pltpu.VMEM((seq_len, SSD_INNER), F32),
            pltpu.VMEM(st_shape, F32),
        ],
        compiler_params=_cparams("parallel"),
        name="ssd_mixer",
    )(*args)


def _swap_pairs(x):
    n = x.shape[-1]
    lane = lax.broadcasted_iota(I32, x.shape, x.ndim - 1)
    return jnp.where(lane % 2 == 0, pltpu.roll(x, n - 1, x.ndim - 1), pltpu.roll(x, 1, x.ndim - 1))


def _kv_prep_kernel(*refs, seq_len, rope):
    if rope:
        (k_ref, v_ref, gain_ref, cos_ref, sin_ref, ck_ref, cv_ref, kt_ref, v4_ref) = refs
    else:
        (k_ref, v_ref, gain_ref, kn_ref, kt_ref, v4_ref) = refs
    kf = k_ref[...].astype(F32)
    parts = []
    for g in range(KV_HEADS):
        kh = kf[:, g * HEAD_DIM:(g + 1) * HEAD_DIM]
        parts.append(kh * lax.rsqrt(jnp.mean(kh * kh, axis=-1, keepdims=True) + EPS))
    kn = jnp.concatenate(parts, axis=1) * gain_ref[...]
    if rope:
        kn = kn * cos_ref[...] + _swap_pairs(kn) * sin_ref[...]
    else:
        kn_ref[...] = kn
    kt_ref[:, 0:seq_len] = kn.T.astype(BF16)
    def with_ones(vh):
        return jnp.concatenate([vh, jnp.ones_like(vh)], axis=1).astype(BF16)

    vf = v_ref[...]
    for g in range(KV_HEADS):
        v4_ref[g, 0:seq_len, :] = with_ones(vf[:, g * HEAD_DIM:(g + 1) * HEAD_DIM])
    if rope:
        past = ck_ref.shape[0]
        kt_ref[:, seq_len:seq_len + past] = ck_ref[...].T.astype(BF16)
        cvf = cv_ref[...]
        for g in range(KV_HEADS):
            v4_ref[g, seq_len:seq_len + past, :] = with_ones(cvf[:, g * HEAD_DIM:(g + 1) * HEAD_DIM])


def kv_prep(main, k_gain, n_seq, seq_len, rope_kv=None, cache=None):
    t = n_seq * seq_len
    rope = rope_kv is not None
    in_specs = [
        pl.BlockSpec((seq_len, KV_INNER), lambda b: (b, M_K // KV_INNER)),
        pl.BlockSpec((seq_len, KV_INNER), lambda b: (b, M_V // KV_INNER)),
        pl.BlockSpec((1, KV_INNER), lambda b: (0, 0)),
    ]
    args = [main, main, k_gain]
    n_keys = seq_len
    out_specs, out_shape = [], []
    if rope:
        cache_k, cache_v, layer = cache
        past = cache_k.shape[2]
        n_keys += past
        in_specs += [
            pl.BlockSpec((seq_len, KV_INNER), lambda b: (0, 0)),
            pl.BlockSpec((seq_len, KV_INNER), lambda b: (0, 0)),
            pl.BlockSpec((None, None, past, KV_INNER), lambda b: (b, layer, 0, 0)),
            pl.BlockSpec((None, None, past, KV_INNER), lambda b: (b, layer, 0, 0)),
        ]
        args += [rope_kv[0], rope_kv[1], cache_k, cache_v]
    else:
        out_specs.append(pl.BlockSpec((seq_len, KV_INNER), lambda b: (b, 0)))
        out_shape.append(jax.ShapeDtypeStruct((t, KV_INNER), F32))
    out_specs += [
        pl.BlockSpec((None, KV_INNER, n_keys), lambda b: (b, 0, 0)),
        pl.BlockSpec((None, KV_HEADS, n_keys, 2 * HEAD_DIM), lambda b: (b, 0, 0, 0)),
    ]
    out_shape += [
        jax.ShapeDtypeStruct((n_seq, KV_INNER, n_keys), BF16),
        jax.ShapeDtypeStruct((n_seq, KV_HEADS, n_keys, 2 * HEAD_DIM), BF16),
    ]
    return pl.pallas_call(
        functools.partial(_kv_prep_kernel, seq_len=seq_len, rope=rope),
        grid=(n_seq,),
        in_specs=in_specs,
        out_specs=out_specs,
        out_shape=out_shape,
        compiler_params=_cparams("parallel"),
        name="kv_prep",
    )(*args)


def _attn_kernel(*refs, rope, tq):
    if rope:
        qa_ref, qb_ref, gain_ref, cos_ref, sin_ref, kt_ref, v4_ref, o_ref = refs
    else:
        qa_ref, qb_ref, gain_ref, kt_ref, v4_ref, o_ref = refs
    rep = N_HEADS // KV_HEADS
    scale = HEAD_DIM ** -0.5
    slabs = ATTN_INNER // LANES // 2
    heads = []
    for j in range(N_HEADS // 2):
        q_ref = qa_ref if j < slabs else qb_ref
        qs = q_ref[:, (j % slabs) * LANES:(j % slabs + 1) * LANES].astype(F32)
        qg = qs * gain_ref[...]
        if rope:
            qg = qg * cos_ref[...] + _swap_pairs(qg) * sin_ref[...]
        for hh in range(2):
            raw = qs[:, hh * HEAD_DIM:(hh + 1) * HEAD_DIM]
            inv = lax.rsqrt(jnp.mean(raw * raw, axis=-1, keepdims=True) + EPS) * scale
            heads.append((qg[:, hh * HEAD_DIM:(hh + 1) * HEAD_DIM] * inv).astype(BF16))
    for g in range(KV_HEADS):
        qstack = jnp.concatenate(heads[g * rep:(g + 1) * rep], axis=0)
        kt = kt_ref[g * HEAD_DIM:(g + 1) * HEAD_DIM, :]
        s = jnp.dot(qstack, kt, preferred_element_type=F32)
        m = jnp.max(s, axis=-1, keepdims=True)
        p = jnp.exp(s - m).astype(BF16)
        ov = jnp.dot(p, v4_ref[g], preferred_element_type=F32)
        o = ov[:, :HEAD_DIM] / ov[:, HEAD_DIM:HEAD_DIM + 1]
        for r in range(rep):
            h = g * rep + r
            o_ref[:, h * HEAD_DIM:(h + 1) * HEAD_DIM] = o[r * tq:(r + 1) * tq, :].astype(o_ref.dtype)


def attention(main, q_gain, kt, v4, n_seq, seq_len, tq, rope_q=None):
    t = n_seq * seq_len
    nq = seq_len // tq
    n_keys = kt.shape[2]
    rope = rope_q is not None
    half = ATTN_INNER // 2
    in_specs = [
        pl.BlockSpec((tq, half), lambda b, i: (b * nq + i, M_Q // half)),
        pl.BlockSpec((tq, half), lambda b, i: (b * nq + i, M_Q // half + 1)),
        pl.BlockSpec((1, LANES), lambda b, i: (0, 0)),
    ]
    args = [main, main, q_gain]
    if rope:
        in_specs += [pl.BlockSpec((tq, LANES), lambda b, i: (i, 0)), pl.BlockSpec((tq, LANES), lambda b, i: (i, 0))]
        args += [rope_q[0], rope_q[1]]
    in_specs += [
        pl.BlockSpec((None, KV_INNER, n_keys), lambda b, i: (b, 0, 0)),
        pl.BlockSpec((None, KV_HEADS, n_keys, 2 * HEAD_DIM), lambda b, i: (b, 0, 0, 0)),
    ]
    args += [kt, v4]
    return pl.pallas_call(
        functools.partial(_attn_kernel, rope=rope, tq=tq),
        grid=(n_seq, nq),
        in_specs=in_specs,
        out_specs=pl.BlockSpec((tq, ATTN_INNER), lambda b, i: (b * nq + i, 0)),
        out_shape=jax.ShapeDtypeStruct((t, ATTN_INNER), BF16),
        compiler_params=_cparams("parallel", "arbitrary"),
        name="attention",
    )(*args)


def _pool_kernel(p_ref, w_ref, scale_ref, o_ref, *, seq_len):
    n_chunks = seq_len // CHUNK
    win = CHUNK + 2 * HALO
    t_loc = lax.broadcasted_iota(I32, (CHUNK, win), 0)
    r_loc = lax.broadcasted_iota(I32, (CHUNK, win), 1) - HALO
    t_col = lax.broadcasted_iota(I32, (CHUNK, 1), 0)

    def chunk(c, carry):
        s = pl.multiple_of(c * CHUNK, CHUNK)
        w = _chunk_window(p_ref, c, n_chunks, seq_len).astype(BF16)
        outs = []
        for g, width in enumerate(POOL_WINDOWS):
            half = width // 2
            d = r_loc - t_loc
            band = jnp.where((d >= -half) & (d < width - half), 1.0, 0.0).astype(BF16)
            wg = w[:, g * POOL_GROUP_DIM:(g + 1) * POOL_GROUP_DIM]
            sums = jnp.dot(band, wg, preferred_element_type=F32)
            tg = t_col + s
            cnt = (jnp.minimum(tg + (width - half), seq_len) - jnp.maximum(tg - half, 0)).astype(F32)
            cur = wg[HALO:HALO + CHUNK, :].astype(F32)
            diff = sums / cnt - cur
            outs.append(jnp.dot(diff.astype(BF16), w_ref[g], preferred_element_type=F32))
        o_ref[pl.ds(s, CHUNK), :] = (jnp.concatenate(outs, axis=1) * scale_ref[...]).astype(o_ref.dtype)
        return carry

    lax.fori_loop(0, n_chunks, chunk, 0)


def pool_mixer(main, pool_w, layer, pool_scale, n_seq, seq_len):
    t = n_seq * seq_len
    return pl.pallas_call(
        functools.partial(_pool_kernel, seq_len=seq_len),
        grid=(n_seq,),
        in_specs=[
            pl.BlockSpec((seq_len, POOL_INNER), lambda b: (b, M_PIN // POOL_INNER)),
            pl.BlockSpec((None, POOL_GROUPS, POOL_GROUP_DIM, POOL_GROUP_DIM), lambda b: (layer, 0, 0, 0)),
            pl.BlockSpec((1, POOL_INNER), lambda b: (0, 0)),
        ],
        out_specs=pl.BlockSpec((seq_len, POOL_INNER), lambda b: (b, 0)),
        out_shape=jax.ShapeDtypeStruct((t, POOL_INNER), BF16),
        compiler_params=_cparams("parallel"),
        name="pool_mixer",
    )(main, pool_w, pool_scale)


def _merge_kernel(ys_ref, ya_ref, yp_ref, g0a_ref, g0b_ref, g1a_ref, g1b_ref, g2a_ref, g2b_ref, ws_ref, wa_ref,
                  wp_ref, o_ref):
    def gate(a_ref, b_ref):
        return jax.nn.sigmoid(jnp.concatenate([a_ref[...], b_ref[...]], axis=1).astype(F32))

    acc = gate(g0a_ref, g0b_ref) * jnp.dot(ys_ref[...], ws_ref[...], preferred_element_type=F32)
    acc = acc + gate(g1a_ref, g1b_ref) * jnp.dot(ya_ref[...], wa_ref[...], preferred_element_type=F32)
    acc = acc + gate(g2a_ref, g2b_ref) * jnp.dot(yp_ref[...], wp_ref[...], preferred_element_type=F32)
    o_ref[...] = acc.astype(o_ref.dtype)


def merge_branches(y_ssd, o_attn, o_pool, main, lw, tm=512):
    t = y_ssd.shape[0]
    half = D_MODEL // 2
    act = lambda: pl.BlockSpec((tm, SSD_INNER), lambda i: (i, 0))
    gate = lambda k: pl.BlockSpec((tm, half), lambda i: (i, M_GATE // half + k))
    wspec = lambda: pl.BlockSpec((None, SSD_INNER, D_MODEL), lambda i: (lw["layer"], 0, 0))
    return pl.pallas_call(
        _merge_kernel,
        grid=(t // tm,),
        in_specs=[act(), act(), act()] + [gate(k) for k in range(6)] + [wspec(), wspec(), wspec()],
        out_specs=pl.BlockSpec((tm, D_MODEL), lambda i: (i, 0)),
        out_shape=jax.ShapeDtypeStruct((t, D_MODEL), BF16),
        compiler_params=_cparams("parallel"),
        name="merge_branches",
    )(y_ssd, o_attn, o_pool, *([main] * 6), lw["w_ssd_out"], lw["w_attn_out"], lw["w_pool_out"])


def _route(logits_t, bias_col):
    scores = jax.nn.sigmoid(logits_t)
    biased = scores + bias_col
    sc = [scores[e:e + 1, :] for e in range(N_EXPERTS)]
    bi = [biased[e:e + 1, :] for e in range(N_EXPERTS)]
    group_scores = []
    for g in range(N_EXPERT_GROUPS):
        a, b, c, d = bi[4 * g:4 * g + 4]
        hi1, lo1 = jnp.maximum(a, b), jnp.minimum(a, b)
        hi2, lo2 = jnp.maximum(c, d), jnp.minimum(c, d)
        top1 = jnp.maximum(hi1, hi2)
        top2 = jnp.maximum(jnp.minimum(hi1, hi2), jnp.maximum(lo1, lo2))
        group_scores.append(top1 + top2)
    best = jnp.zeros_like(group_scores[0], dtype=I32)
    best_v = group_scores[0]
    for g in range(1, N_EXPERT_GROUPS):
        upd = group_scores[g] > best_v
        best = jnp.where(upd, g, best)
        best_v = jnp.where(upd, group_scores[g], best_v)
    vb, vs = [], []
    for j in range(EXPERTS_PER_GROUP):
        b_j, s_j = bi[j], sc[j]
        for g in range(1, N_EXPERT_GROUPS):
            sel = best == g
            b_j = jnp.where(sel, bi[4 * g + j], b_j)
            s_j = jnp.where(sel, sc[4 * g + j], s_j)
        vb.append(b_j)
        vs.append(s_j)

    def first_argmax(vals, excluded=None):
        idx = None
        for j in range(EXPERTS_PER_GROUP):
            v = vals[j] if excluded is None else jnp.where(excluded == j, -jnp.inf, vals[j])
            if idx is None:
                idx, cur = jnp.zeros_like(best), v
            else:
                upd = v > cur
                idx = jnp.where(upd, j, idx)
                cur = jnp.where(upd, v, cur)
        return idx

    i0 = first_argmax(vb)
    i1 = first_argmax(vb, excluded=i0)

    def pick(vals, idx):
        out = vals[0]
        for j in range(1, EXPERTS_PER_GROUP):
            out = jnp.where(idx == j, vals[j], out)
        return out

    s0, s1 = pick(vs, i0), pick(vs, i1)
    tot = s0 + s1
    ids = jnp.concatenate([best * EXPERTS_PER_GROUP + i0, best * EXPERTS_PER_GROUP + i1], axis=0)
    wts = jnp.concatenate([s0 / tot, s1 / tot], axis=0)
    return ids, wts


def _outproj_kernel(x_ref, m_ref, wo_ref, g1_ref, gain_ref, sh_ref, sc_ref, wr_ref, wr_hi_ref, br_ref,
                    xo_ref, h_ref, ids_ref, rank_ref, wcol_ref, cnt_ref, run_ref):
    tm = x_ref.shape[0]

    @pl.when(pl.program_id(0) == 0)
    def _():
        run_ref[...] = jnp.zeros_like(run_ref)

    mix = jnp.dot(m_ref[...], wo_ref[...], preferred_element_type=F32)
    xn = x_ref[...] + g1_ref[...] * mix
    xo_ref[...] = xn
    h = _rms(xn) * gain_ref[...] * (1.0 + sc_ref[...]) + sh_ref[...]
    h_hi, h_lo = _split_bf16(h)
    for j in range(ROW_TILES):
        h_ref[pl.ds(j, tm, stride=ROW_TILES), :] = h[:, j * LANES:(j + 1) * LANES]
    lg = jnp.dot(h_hi, wr_ref[...], preferred_element_type=F32) + jnp.dot(h_lo, wr_hi_ref[...],
                                                                         preferred_element_type=F32)
    lg_t = lg.T
    logits_t = lg_t[0:N_EXPERTS, :] + lg_t[N_EXPERTS:2 * N_EXPERTS, :]
    ids, wts = _route(logits_t, br_ref[...])

    def fold(rows):
        return jnp.concatenate([rows[k:k + 1, j * LANES:(j + 1) * LANES] for k in range(2)
                                for j in range(tm // LANES)], axis=0)

    ids_ref[...] = fold(ids)
    e_iota = lax.broadcasted_iota(I32, (N_EXPERTS, tm), 0)
    earlier = (lax.broadcasted_iota(I32, (tm, tm), 0) < lax.broadcasted_iota(I32, (tm, tm), 1))
    earlier = jnp.where(earlier, 1.0, 0.0).astype(BF16)
    oh0 = jnp.where(e_iota == ids[0:1, :], 1.0, 0.0)
    oh1 = jnp.where(e_iota == ids[1:2, :], 1.0, 0.0)
    before0 = jnp.dot(oh0.astype(BF16), earlier, preferred_element_type=F32)
    before1 = jnp.dot(oh1.astype(BF16), earlier, preferred_element_type=F32)
    c0 = jnp.sum(oh0, axis=1, keepdims=True)
    c1 = jnp.sum(oh1, axis=1, keepdims=True)
    running = run_ref[...]
    rank0 = jnp.sum(oh0 * (before0 + running), axis=0, keepdims=True)
    rank1 = jnp.sum(oh1 * (before1 + running + c0), axis=0, keepdims=True)
    rank_ref[...] = fold(jnp.concatenate([rank0, rank1], axis=0).astype(I32))
    running = running + c0 + c1
    run_ref[...] = running
    cnt_ref[...] = jnp.broadcast_to(running, cnt_ref.shape)
    r_iota = lax.broadcasted_iota(I32, (LANES, tm), 0)
    w_rows = jnp.where(r_iota == 0, wts[0:1, :], jnp.where(r_iota == 1, wts[1:2, :], 0.0))
    wcol_ref[...] = w_rows.T


def outproj_route(x, merged, mods, lw, gw, path):
    t = x.shape[0]
    tm = ROUTE_TILE
    rf = _row_fn(path, tm)
    row = lambda: pl.BlockSpec((1, D_MODEL), lambda i: (0, 0))
    return pl.pallas_call(
        _outproj_kernel,
        grid=(t // tm,),
        in_specs=[
            pl.BlockSpec((tm, D_MODEL), lambda i: (i, 0)),
            pl.BlockSpec((tm, D_MODEL), lambda i: (i, 0)),
            pl.BlockSpec((None, D_MODEL, D_MODEL), lambda i: (lw["layer"], 0, 0)),
            _mod_spec(2, rf),
            row(),
            _mod_spec(3, rf),
            _mod_spec(4, rf),
            pl.BlockSpec((D_MODEL, LANES), lambda i: (0, 0)),
            pl.BlockSpec((D_MODEL, LANES), lambda i: (0, 0)),
            pl.BlockSpec((N_EXPERTS, 1), lambda i: (0, 0)),
        ],
        out_specs=[
            pl.BlockSpec((tm, D_MODEL), lambda i: (i, 0)),
            pl.BlockSpec((tm * ROW_TILES, LANES), lambda i: (i, 0)),
            pl.BlockSpec((None, 2 * tm // LANES, LANES), lambda i: (i, 0, 0)),
            pl.BlockSpec((None, 2 * tm // LANES, LANES), lambda i: (i, 0, 0)),
            pl.BlockSpec((tm, LANES), lambda i: (i, 0)),
            pl.BlockSpec((N_EXPERTS, LANES), lambda i: (0, 0)),
        ],
        out_shape=[
            jax.ShapeDtypeStruct((t, D_MODEL), F32),
            jax.ShapeDtypeStruct((t * ROW_TILES, LANES), F32),
            jax.ShapeDtypeStruct((t // tm, 2 * tm // LANES, LANES), I32),
            jax.ShapeDtypeStruct((t // tm, 2 * tm // LANES, LANES), I32),
            jax.ShapeDtypeStruct((t, LANES), F32),
            jax.ShapeDtypeStruct((N_EXPERTS, LANES), F32),
        ],
        scratch_shapes=[pltpu.VMEM((N_EXPERTS, 1), F32)],
        compiler_params=_cparams("arbitrary"),
        name="outproj_route",
    )(x, merged, lw["w_out"], mods, lw["norm_ffn"], mods, mods, gw["w_router_cat"], gw["w_router_hi"],
      gw["b_router"])


def _moe_kernel(tile_expert_ref, src_ref, n_used_ref, h_hbm, wg_ref, wu_ref, wd_ref, o_ref, xbuf0, xbuf1, sem):
    i = pl.program_id(0)
    n_used = n_used_ref[0]
    bufs = (xbuf0, xbuf1)

    def start_tile(tile, slot):
        for r in range(MOE_TILE):
            row = pl.multiple_of(src_ref[tile * MOE_TILE + r] * ROW_TILES, ROW_TILES)
            pltpu.make_async_copy(h_hbm.at[pl.ds(row, ROW_TILES)], bufs[slot].at[pl.ds(r * ROW_TILES, ROW_TILES)],
                                  sem.at[slot]).start(priority=r % 2)

    def wait_tile(slot):
        pltpu.make_async_copy(h_hbm.at[pl.ds(0, MOE_TILE * ROW_TILES)], bufs[slot], sem.at[slot]).wait()

    def compute(slot):
        x = jnp.concatenate([bufs[slot][pl.ds(j, MOE_TILE, stride=ROW_TILES), :].astype(BF16)
                             for j in range(ROW_TILES)], axis=1)
        a = jnp.dot(x, wg_ref[...], preferred_element_type=F32)
        u = jnp.dot(x, wu_ref[...], preferred_element_type=F32)
        act = _silu(a) * u
        o_ref[...] = jnp.dot(act.astype(BF16), wd_ref[...], preferred_element_type=F32).astype(o_ref.dtype)

    @pl.when(i == 0)
    def _():
        start_tile(0, 0)

    for slot in range(2):
        @pl.when((i % 2 == slot) & (i + 1 < n_used))
        def _(slot=slot):
            wait_tile(slot)
            start_tile(i + 1, 1 - slot)
            compute(slot)

        @pl.when((i % 2 == slot) & (i + 1 == n_used))
        def _(slot=slot):
            wait_tile(slot)
            compute(slot)

    @pl.when(i >= n_used)
    def _():
        o_ref[...] = jnp.zeros_like(o_ref)


def moe_experts(h, src, tile_expert, n_used, lw):
    rows = src.shape[0]
    n_tiles = rows // MOE_TILE
    layer = lw["layer"]
    grid_spec = pltpu.PrefetchScalarGridSpec(
        num_scalar_prefetch=3,
        grid=(n_tiles,),
        in_specs=[
            pl.BlockSpec(memory_space=pl.ANY),
            pl.BlockSpec((None, None, D_MODEL, D_FF), lambda i, te, sr, nu: (layer, te[i], 0, 0)),
            pl.BlockSpec((None, None, D_MODEL, D_FF), lambda i, te, sr, nu: (layer, te[i], 0, 0)),
            pl.BlockSpec((None, None, D_FF, D_MODEL), lambda i, te, sr, nu: (layer, te[i], 0, 0)),
        ],
        out_specs=pl.BlockSpec((MOE_TILE, D_MODEL), lambda i, te, sr, nu: (i, 0)),
        scratch_shapes=[pltpu.VMEM((MOE_TILE * ROW_TILES, LANES), F32), pltpu.VMEM((MOE_TILE * ROW_TILES, LANES), F32),
                        pltpu.SemaphoreType.DMA((2,))],
    )
    return pl.pallas_call(
        _moe_kernel,
        grid_spec=grid_spec,
        out_shape=jax.ShapeDtypeStruct((rows, D_MODEL), BF16),
        compiler_params=_cparams("arbitrary"),
        name="moe_experts",
    )(tile_expert, src, n_used, h, lw["w_gate_ff"], lw["w_up_ff"], lw["w_down_ff"])


def _moe_plan(ids, rank, counts):
    two_t = ids.size
    ids = ids.reshape(two_t)
    rank = rank.reshape(two_t)
    counts = counts[:, 0].astype(I32)
    padded = ((counts + MOE_TILE - 1) // MOE_TILE) * MOE_TILE
    ends = jnp.cumsum(padded)
    starts = ends - padded
    onehot = ids[:, None] == jnp.arange(N_EXPERTS, dtype=I32)[None, :]
    pos = jnp.sum(jnp.where(onehot, starts[None, :], 0), axis=1) + rank
    n_rows = two_t + N_EXPERTS * MOE_TILE
    n_tiles = n_rows // MOE_TILE
    pair = jnp.arange(two_t, dtype=I32)
    token = (pair // (2 * ROUTE_TILE)) * ROUTE_TILE + pair % ROUTE_TILE
    src = jnp.zeros((n_rows,), I32).at[pos].set(token, mode="promise_in_bounds", unique_indices=True)
    tile_start = jnp.arange(n_tiles, dtype=I32) * MOE_TILE
    tile_expert = jnp.minimum(jnp.sum((tile_start[:, None] >= ends[None, :]).astype(I32), axis=1), N_EXPERTS - 1)
    n_used = (ends[-1] // MOE_TILE).reshape(1).astype(I32)
    return pos, src, tile_expert.astype(I32), n_used


def moe_ffn(h2, ids, rank, counts, lw):
    pos, src, tile_expert, n_used = _moe_plan(ids, rank, counts)
    ys = moe_experts(h2, src, tile_expert, n_used, lw)
    return ys.at[pos].get(mode="promise_in_bounds", unique_indices=True)


def _rope_tables(n_tokens):
    rows = n_tokens // GRID_W
    row = jnp.repeat(jnp.arange(rows, dtype=F32), GRID_W)
    col = jnp.tile(jnp.arange(GRID_W, dtype=F32), rows)
    n_freq = HEAD_DIM // 4
    inv = ROPE_THETA ** (-jnp.arange(n_freq, dtype=F32) / n_freq)
    ang = jnp.concatenate([row[:, None] * inv, col[:, None] * inv], axis=-1)
    cos = jnp.repeat(jnp.cos(ang), 2, axis=-1)
    sin = jnp.repeat(jnp.sin(ang), 2, axis=-1)
    sign = jnp.tile(jnp.array([-1.0, 1.0], F32), HEAD_DIM // 2)
    sin = sin * sign
    return cos, sin


def _prep_stacked_weights(p):
    w_in = p["w_in"]
    depth = w_in.shape[0]
    w_main = jnp.concatenate([w_in[:, :, :_O_DTF], w_in[:, :, _O_Q:]], axis=2).astype(BF16)
    w_small = jnp.concatenate([w_in[:, :, _O_DTF:_O_Q],
                               jnp.zeros((depth, D_MODEL, LANES - 2 * SSD_HEADS), F32)], axis=2).astype(BF16)
    out = {"w_main": w_main, "w_small": w_small}
    for name in ("pool_w", "w_ssd_out", "w_attn_out", "w_pool_out", "w_out", "w_gate_ff", "w_up_ff", "w_down_ff"):
        out[name] = p[name].astype(BF16)
    return out


def _prep_layer_weights(l, p, stacked):
    pad_lanes = lambda v: jnp.concatenate([v.reshape(-1), jnp.zeros((LANES - v.size,), F32)]).reshape(1, LANES)
    return {
        **stacked,
        "layer": l,
        "conv_w": jnp.concatenate([p["conv_w"][l], jnp.zeros((8 - CONV_W, XBC_DIM), F32)], axis=0),
        "conv_b": p["conv_b"][l].reshape(1, XBC_DIM),
        "dt_bias": pad_lanes(p["dt_bias"][l]),
        "a_log": pad_lanes(p["a_log"][l]),
        "d_skip": jnp.repeat(p["d_skip"][l], SSD_HEADDIM).reshape(1, SSD_INNER),
        "ssd_norm": p["ssd_norm"][l].reshape(1, SSD_INNER),
        "q_gain": jnp.tile(p["q_norm"][l], LANES // HEAD_DIM).reshape(1, LANES),
        "k_gain": jnp.tile(p["k_norm"][l], KV_HEADS).reshape(1, KV_INNER),
        "pool_scale": p["pool_scale"][l].reshape(1, POOL_INNER),
        "norm_mix": p["norm_mix"][l].reshape(1, D_MODEL),
        "norm_ffn": p["norm_ffn"][l].reshape(1, D_MODEL),
    }


def _mixer_stage(h, x, mods, lw, gw, path, n_seq, seq_len, tq, h0, rope, cache):
    t = n_seq * seq_len
    main = matmul(h, lw["w_main"], lw["layer"], BF16, min(t, 2048), MAIN_TN, "in_proj_main")
    small = matmul(h, lw["w_small"], lw["layer"], F32, min(t, 2048), LANES, "in_proj_dt")
    y_ssd, hf, hb = ssd_mixer(main, small, h0[0], h0[1], lw, n_seq, seq_len)
    if rope is None:
        k_norm, kt, v4 = kv_prep(main, lw["k_gain"], n_seq, seq_len)
        o_attn = attention(main, lw["q_gain"], kt, v4, n_seq, seq_len, tq)
    else:
        k_norm = None
        kt, v4 = kv_prep(main, lw["k_gain"], n_seq, seq_len, rope_kv=rope[0], cache=cache)
        o_attn = attention(main, lw["q_gain"], kt, v4, n_seq, seq_len, tq, rope_q=rope[1])
    o_pool = pool_mixer(main, lw["pool_w"], lw["layer"], lw["pool_scale"], n_seq, seq_len)
    merged = merge_branches(y_ssd, o_attn, o_pool, main, lw)
    x_new, h2, ids, rank, wcol, counts = outproj_route(x, merged, mods, lw, gw, path)
    y2 = moe_ffn(h2, ids, rank, counts, lw)
    return x_new, y2, wcol, k_norm, main[:, M_V:M_V + KV_INNER].astype(F32), hf, hb


def kernel(x_prompt, x_sample, cache_k, cache_v, state_ssd_fwd, state_ssd_bwd, c, c_ctx, w_ada, b_ada, norm_mix,
           norm_ffn, w_in, conv_w, conv_b, a_log, dt_bias, d_skip, ssd_norm, q_norm, k_norm, pool_w, pool_scale,
           w_ssd_out, w_attn_out, w_pool_out, w_out, w_router, b_router, w_gate_ff, w_up_ff, w_down_ff, norm_final):
    p = dict(norm_mix=norm_mix, norm_ffn=norm_ffn, w_in=w_in, conv_w=conv_w, conv_b=conv_b, a_log=a_log,
             dt_bias=dt_bias, d_skip=d_skip, ssd_norm=ssd_norm, q_norm=q_norm, k_norm=k_norm, pool_w=pool_w,
             pool_scale=pool_scale, w_ssd_out=w_ssd_out, w_attn_out=w_attn_out, w_pool_out=w_pool_out, w_out=w_out,
             w_gate_ff=w_gate_ff, w_up_ff=w_up_ff, w_down_ff=w_down_ff)
    bp, lc, _ = x_prompt.shape
    depth = w_in.shape[0]
    bs, ls, _ = x_sample.shape
    past = cache_k.shape[2]
    tc, tl = bp * lc, bs * ls

    mod_rows = -(-(1 + bs) // 8) * 8
    cond = jnp.concatenate([c_ctx[None, :], c, jnp.zeros((mod_rows - 1 - bs, D_MODEL), F32)], axis=0)
    mods_all = ada_mods(cond, w_ada, b_ada).reshape(depth, mod_rows, 6, 1, D_MODEL)

    wr_hi, wr_lo = _split_bf16(w_router)
    zpad = jnp.zeros((D_MODEL, LANES - 2 * N_EXPERTS), BF16)
    gw = {
        "w_router_cat": jnp.concatenate([wr_hi, wr_lo, zpad], axis=1),
        "w_router_hi": jnp.concatenate([wr_hi, jnp.zeros((D_MODEL, LANES - N_EXPERTS), BF16)], axis=1),
        "b_router": b_router.reshape(N_EXPERTS, 1),
    }
    cos, sin = _rope_tables(ls)
    rope = ((jnp.tile(cos, (1, KV_HEADS)), jnp.tile(sin, (1, KV_HEADS))),
            (jnp.tile(cos, (1, LANES // HEAD_DIM)), jnp.tile(sin, (1, LANES // HEAD_DIM))))
    cache_k4 = cache_k.reshape(bs, depth, past, KV_INNER)
    cache_v4 = cache_v.reshape(bs, depth, past, KV_INNER)
    pairs = SSD_HEADS // 2

    def pack_state(st):
        st = st.reshape(bs, depth, pairs, 2, SSD_HEADDIM, D_STATE)
        return st.transpose(0, 1, 2, 5, 3, 4).reshape(bs, depth, pairs, D_STATE, 2 * SSD_HEADDIM)

    def unpack_state(st):
        return st.reshape(st.shape[0], SSD_HEADS, SSD_HEADDIM, D_STATE)

    st_f = pack_state(state_ssd_fwd)
    st_b = pack_state(state_ssd_bwd)

    path_c = (0, False, lc)
    path_l = (1, True, ls)
    xc = x_prompt.reshape(tc, D_MODEL)
    xl = x_sample.reshape(tl, D_MODEL)
    ks, vs, hfs, hbs = [], [], [], []
    hc = hl = None
    y2c = y2l = wc = wl = None
    stacked = _prep_stacked_weights(p)
    for l in range(depth):
        lw = _prep_layer_weights(l, p, stacked)
        mods = mods_all[l]
        if l == 0:
            hc = modulate_first(xc, mods, lw["norm_mix"], path_c)
            hl = modulate_first(xl, mods, lw["norm_mix"], path_l)
        else:
            xc, hc = residual_modulate(xc, y2c, wc, mods_all[l - 1], mods, lw["norm_mix"], path_c)
            xl, hl = residual_modulate(xl, y2l, wl, mods_all[l - 1], mods, lw["norm_mix"], path_l)
        xc, y2c, wc, k_c, v_c, hf, hb = _mixer_stage(
            hc, xc, mods, lw, gw, path_c, bp, lc, lc, (None, None), None, None)
        ks.append(k_c.reshape(bp, lc, KV_HEADS, HEAD_DIM))
        vs.append(v_c.reshape(bp, lc, KV_HEADS, HEAD_DIM))
        hfs.append(unpack_state(hf))
        hbs.append(unpack_state(hb))
        xl, y2l, wl, _, _, _, _ = _mixer_stage(
            hl, xl, mods, lw, gw, path_l, bs, ls, CHUNK, (st_f[:, l], st_b[:, l]), rope, (cache_k4, cache_v4, l))
    gain_f = norm_final.reshape(1, D_MODEL)
    y_prompt = residual_final(xc, y2c, wc, mods_all[depth - 1], gain_f, path_c).reshape(bp, lc, D_MODEL)
    y_sample = residual_final(xl, y2l, wl, mods_all[depth - 1], gain_f, path_l).reshape(bs, ls, D_MODEL)
    return (y_prompt, y_sample, jnp.stack(ks, axis=1), jnp.stack(vs, axis=1), jnp.stack(hfs, axis=1),
            jnp.stack(hbs, axis=1))
```

```python
import functools

import jax
import jax.numpy as jnp
from jax import lax
from jax.experimental import pallas as pl
from jax.experimental.pallas import tpu as pltpu

F32 = jnp.float32
BF16 = jnp.bfloat16
I32 = jnp.int32

D_MODEL = 2048
GRID_W = 64
EPS = 1e-6

SSD_HEADS = 16
SSD_HEADDIM = 64
SSD_INNER = SSD_HEADS * SSD_HEADDIM
SSD_GROUPS = 4
D_STATE = 64
CONV_W = 5
CHUNK = 128
XBC_DIM = SSD_INNER + 2 * SSD_GROUPS * D_STATE

N_HEADS = 16
KV_HEADS = 4
HEAD_DIM = 64
ATTN_INNER = N_HEADS * HEAD_DIM
KV_INNER = KV_HEADS * HEAD_DIM
ROPE_THETA = 10000.0

POOL_GROUPS = 4
POOL_INNER = 1024
POOL_GROUP_DIM = POOL_INNER // POOL_GROUPS
POOL_WINDOWS = (2, 4, 8, 16)

N_EXPERTS = 16
N_EXPERT_GROUPS = 4
EXPERTS_PER_GROUP = N_EXPERTS // N_EXPERT_GROUPS
D_FF = 512

_O_Z = 0
_O_XBC = _O_Z + SSD_INNER
_O_DTF = _O_XBC + XBC_DIM
_O_DTB = _O_DTF + SSD_HEADS
_O_Q = _O_DTB + SSD_HEADS
_O_K = _O_Q + ATTN_INNER
_O_V = _O_K + KV_INNER
_O_PIN = _O_V + KV_INNER
_O_GATE = _O_PIN + POOL_INNER

M_Z = 0
M_XBC = M_Z + SSD_INNER
M_Q = M_XBC + XBC_DIM
M_K = M_Q + ATTN_INNER
M_V = M_K + KV_INNER
M_PIN = M_V + KV_INNER
M_GATE = M_PIN + POOL_INNER
MAIN_N = M_GATE + 3 * D_MODEL
MAIN_TN = 1024
LANES = 128

HALO = 16
MOE_TILE = 256
MOE_AHEAD = 2
MOE_BUFS = MOE_AHEAD + 1
ROUTE_TILE = 512
ROW_TILES = D_MODEL // LANES
VMEM_LIMIT = 56 * 1024 * 1024


def _cparams(*sem):
    return pltpu.CompilerParams(dimension_semantics=sem, vmem_limit_bytes=VMEM_LIMIT)


def _silu(x):
    return x * jax.nn.sigmoid(x)


def _rms(xf):
    return xf * lax.rsqrt(jnp.mean(xf * xf, axis=-1, keepdims=True) + EPS)


def _split_bf16(x):
    hi = x.astype(BF16)
    lo = (x - hi.astype(F32)).astype(BF16)
    return hi, lo


def _ada_kernel(cond_ref, w_ref, b_ref, o_ref):
    c = _silu(cond_ref[...])
    o_ref[...] = jnp.dot(c.astype(BF16), w_ref[...].astype(BF16), preferred_element_type=F32) + b_ref[...]


def ada_mods(cond, w_ada, b_ada, tn=1024):
    rows = cond.shape[0]
    depth, _, n = w_ada.shape
    return pl.pallas_call(
        _ada_kernel,
        grid=(depth, n // tn),
        in_specs=[
            pl.BlockSpec((rows, D_MODEL), lambda l, j: (0, 0)),
            pl.BlockSpec((None, D_MODEL, tn), lambda l, j: (l, 0, j)),
            pl.BlockSpec((None, 1, tn), lambda l, j: (l, 0, j)),
        ],
        out_specs=pl.BlockSpec((None, rows, tn), lambda l, j: (l, 0, j)),
        out_shape=jax.ShapeDtypeStruct((depth, rows, n), F32),
        compiler_params=_cparams("parallel", "parallel"),
        name="ada_mods",
    )(cond, w_ada, b_ada.reshape(depth, 1, n))


def _mod_spec(which, row_of_tile):
    return pl.BlockSpec((None, None, 1, D_MODEL), lambda i: (row_of_tile(i), which, 0, 0))


def _row_fn(path, tm):
    row0, per_seq, seq_len = path
    if per_seq:
        return lambda i: row0 + (i * tm) // seq_len
    return lambda i: row0


def _modulate_kernel(x_ref, g_ref, sh_ref, sc_ref, h_ref):
    xf = x_ref[...]
    h_ref[...] = (_rms(xf) * g_ref[...] * (1.0 + sc_ref[...]) + sh_ref[...]).astype(h_ref.dtype)


def modulate_first(x, mods, gain, path, tm=512):
    t = x.shape[0]
    rf = _row_fn(path, tm)
    return pl.pallas_call(
        _modulate_kernel,
        grid=(t // tm,),
        in_specs=[
            pl.BlockSpec((tm, D_MODEL), lambda i: (i, 0)),
            pl.BlockSpec((1, D_MODEL), lambda i: (0, 0)),
            _mod_spec(0, rf),
            _mod_spec(1, rf),
        ],
        out_specs=pl.BlockSpec((tm, D_MODEL), lambda i: (i, 0)),
        out_shape=jax.ShapeDtypeStruct((t, D_MODEL), BF16),
        compiler_params=_cparams("parallel"),
        name="modulate_first",
    )(x, gain, mods, mods)


def _moe_residual(x_ref, ya_ref, yb_ref, w_ref, g2_ref):
    w = w_ref[...]
    y = w[:, 0:1] * ya_ref[...].astype(F32) + w[:, 1:2] * yb_ref[...].astype(F32)
    return x_ref[...] + g2_ref[...] * y


def _residual_modulate_kernel(x_ref, ya_ref, yb_ref, w_ref, g2_ref, gain_ref, sh_ref, sc_ref, xo_ref, h_ref):
    xn = _moe_residual(x_ref, ya_ref, yb_ref, w_ref, g2_ref)
    xo_ref[...] = xn
    h_ref[...] = (_rms(xn) * gain_ref[...] * (1.0 + sc_ref[...]) + sh_ref[...]).astype(h_ref.dtype)


def residual_modulate(x, y2, wcol, mods_prev, mods_next, gain_next, path):
    t = x.shape[0]
    tm = ROUTE_TILE
    nt = t // tm
    rf = _row_fn(path, tm)
    return pl.pallas_call(
        _residual_modulate_kernel,
        grid=(nt,),
        in_specs=[
            pl.BlockSpec((tm, D_MODEL), lambda i: (i, 0)),
            pl.BlockSpec((tm, D_MODEL), lambda i: (2 * i, 0)),
            pl.BlockSpec((tm, D_MODEL), lambda i: (2 * i + 1, 0)),
            pl.BlockSpec((tm, LANES), lambda i: (i, 0)),
            _mod_spec(5, rf),
            pl.BlockSpec((1, D_MODEL), lambda i: (0, 0)),
            _mod_spec(0, rf),
            _mod_spec(1, rf),
        ],
        out_specs=[
            pl.BlockSpec((tm, D_MODEL), lambda i: (i, 0)),
            pl.BlockSpec((tm, D_MODEL), lambda i: (i, 0)),
        ],
        out_shape=[
            jax.ShapeDtypeStruct((t, D_MODEL), F32),
            jax.ShapeDtypeStruct((t, D_MODEL), BF16),
        ],
        compiler_params=_cparams("parallel"),
        name="residual_modulate",
    )(x, y2, y2, wcol, mods_prev, gain_next, mods_next, mods_next)


def _residual_final_kernel(x_ref, ya_ref, yb_ref, w_ref, g2_ref, gain_ref, y_ref):
    xn = _moe_residual(x_ref, ya_ref, yb_ref, w_ref, g2_ref)
    y_ref[...] = _rms(xn) * gain_ref[...]


def residual_final(x, y2, wcol, mods_prev, gain, path):
    t = x.shape[0]
    tm = ROUTE_TILE
    nt = t // tm
    rf = _row_fn(path, tm)
    return pl.pallas_call(
        _residual_final_kernel,
        grid=(nt,),
        in_specs=[
            pl.BlockSpec((tm, D_MODEL), lambda i: (i, 0)),
            pl.BlockSpec((tm, D_MODEL), lambda i: (2 * i, 0)),
            pl.BlockSpec((tm, D_MODEL), lambda i: (2 * i + 1, 0)),
            pl.BlockSpec((tm, LANES), lambda i: (i, 0)),
            _mod_spec(5, rf),
            pl.BlockSpec((1, D_MODEL), lambda i: (0, 0)),
        ],
        out_specs=pl.BlockSpec((tm, D_MODEL), lambda i: (i, 0)),
        out_shape=jax.ShapeDtypeStruct((t, D_MODEL), F32),
        compiler_params=_cparams("parallel"),
        name="residual_final",
    )(x, y2, y2, wcol, mods_prev, gain)


def _mm_kernel(a_ref, b_ref, o_ref):
    o_ref[...] = jnp.dot(a_ref[...], b_ref[...], preferred_element_type=F32).astype(o_ref.dtype)


def matmul(a, b, layer, out_dtype, tm, tn, name):
    m, k = a.shape
    n = b.shape[2]
    return pl.pallas_call(
        _mm_kernel,
        grid=(m // tm, n // tn),
        in_specs=[
            pl.BlockSpec((tm, k), lambda i, j: (i, 0)),
            pl.BlockSpec((None, k, tn), lambda i, j: (layer, 0, j)),
        ],
        out_specs=pl.BlockSpec((tm, tn), lambda i, j: (i, j)),
        out_shape=jax.ShapeDtypeStruct((m, n), out_dtype),
        compiler_params=_cparams("parallel", "arbitrary"),
        name=name,
    )(a, b)


def _chunk_window(ref, c, n_chunks, seq_len):
    s = pl.multiple_of(c * CHUNK, CHUNK)
    cur = ref[pl.ds(s, CHUNK), :].astype(F32)
    sp = pl.multiple_of(jnp.maximum(s - HALO, 0), HALO)
    sn = pl.multiple_of(jnp.minimum(s + CHUNK, seq_len - HALO), HALO)
    prev = ref[pl.ds(sp, HALO), :].astype(F32)
    nxt = ref[pl.ds(sn, HALO), :].astype(F32)
    prev = jnp.where(c > 0, prev, 0.0)
    nxt = jnp.where(c < n_chunks - 1, nxt, 0.0)
    return jnp.concatenate([prev, cur, nxt], axis=0)


def _ssd_kernel(*refs, seq_len, has_h0):
    if has_h0:
        (zx_ref, dt_ref, h0f_ref, h0b_ref, cw_ref, cb_ref, dtb_ref, alog_ref, dskip_ref, gain_ref,
         expand_ref, colsel_ref, y_ref, hf_ref, hb_ref, xc_ref, sp_ref, yacc_ref, st_ref) = refs
    else:
        (zx_ref, dt_ref, cw_ref, cb_ref, dtb_ref, alog_ref, dskip_ref, gain_ref,
         expand_ref, colsel_ref, _, _, y_ref, hf_ref, hb_ref, xc_ref, sp_ref, yacc_ref, st_ref) = refs
        h0f_ref = h0b_ref = None
    z_ref = zx_ref.at[:, pl.ds(M_Z, SSD_INNER)]
    xbc_ref = zx_ref.at[:, pl.ds(M_XBC, XBC_DIM)]
    n_chunks = seq_len // CHUNK
    win = CHUNK + 2 * HALO

    def conv_chunk(c, carry):
        w = _chunk_window(xbc_ref, c, n_chunks, seq_len)
        acc = jnp.zeros((CHUNK, XBC_DIM), F32) + cb_ref[...]
        for j in range(CONV_W):
            shift = (CONV_W // 2 - j) % win
            wj = w if shift == 0 else pltpu.roll(w, shift, 0)
            acc = acc + cw_ref[j:j + 1, :] * wj[HALO:HALO + CHUNK, :]
        s = pl.multiple_of(c * CHUNK, CHUNK)
        xc_ref[pl.ds(s, CHUNK), :] = _silu(acc).astype(BF16)
        return carry

    lax.fori_loop(0, n_chunks, conv_chunk, 0)

    xdt = dt_ref[...] + dtb_ref[...]
    sp_ref[...] = jnp.maximum(xdt, 0.0) + jnp.log1p(jnp.exp(-jnp.abs(xdt)))
    a_row = -jnp.exp(alog_ref[...])

    rows = lax.broadcasted_iota(I32, (CHUNK, CHUNK), 0)
    cols = lax.broadcasted_iota(I32, (CHUNK, CHUNK), 1)
    lane_lo = cols < SSD_HEADDIM

    def cat2(x):
        hi, lo = _split_bf16(x)
        return jnp.concatenate([hi, lo], axis=1)

    def scan_chunk(c, forward):
        d = 0 if forward else 1
        off = d * SSD_HEADS
        s = pl.multiple_of(c * CHUNK, CHUNK)
        xcv = xc_ref[pl.ds(s, CHUNK), :]
        dtc = sp_ref[pl.ds(s, CHUNK), :]
        a = dtc * a_row
        tri = (rows >= cols) if forward else (cols >= rows)
        tri_b = jnp.where(tri, 1.0, 0.0).astype(BF16)
        a1 = a.astype(BF16)
        r1 = a - a1.astype(F32)
        a2 = r1.astype(BF16)
        a3 = (r1 - a2.astype(F32)).astype(BF16)
        cs = (jnp.dot(tri_b, a1, preferred_element_type=F32) + jnp.dot(tri_b, a2, preferred_element_type=F32)
              + jnp.dot(tri_b, a3, preferred_element_type=F32))
        cs_t = cs.T
        tot = cs[CHUNK - 1:CHUNK, :] if forward else cs[0:1, :]
        expand = expand_ref[d]
        dt_full = jnp.dot(cat2(dtc), expand, preferred_element_type=F32)
        dout_full = jnp.dot(cat2(jnp.exp(cs)), expand, preferred_element_type=F32)
        dst_full = jnp.dot(cat2(jnp.exp(tot - cs)), expand, preferred_element_type=F32)
        etot_full = dout_full[CHUNK - 1:CHUNK, :] if forward else dout_full[0:1, :]
        colb = jnp.dot(cat2(cs), colsel_ref[d], preferred_element_type=F32)
        xd = xcv[:, :SSD_INNER].astype(F32) * dt_full
        xdb = xd.astype(BF16)
        xdd = (xd * dst_full).astype(BF16)
        b_t = xcv[:, SSD_INNER:SSD_INNER + SSD_GROUPS * D_STATE].astype(F32).T.astype(BF16)
        zero = jnp.zeros((CHUNK, CHUNK), BF16)
        ys = []
        for g in range(SSD_GROUPS):
            bg_t = b_t[g * D_STATE:(g + 1) * D_STATE, :]
            cg = xcv[:, SSD_INNER + (SSD_GROUPS + g) * D_STATE:SSD_INNER + (SSD_GROUPS + g + 1) * D_STATE]
            gmat = jnp.dot(cg, bg_t, preferred_element_type=F32)
            for pp in range(2):
                j = 2 * g + pp
                ms = []
                for hh in range(2):
                    h = 2 * j + hh
                    diff = colb[:, h * CHUNK:(h + 1) * CHUNK] - cs_t[off + h:off + h + 1, :]
                    lm = jnp.exp(jnp.where(tri, diff, -jnp.inf))
                    ms.append((gmat * lm).astype(BF16))
                mcat = jnp.concatenate(ms, axis=1)
                xp = xdb[:, j * CHUNK:(j + 1) * CHUNK]
                xblk = jnp.concatenate([jnp.where(lane_lo, xp, zero), jnp.where(lane_lo, zero, xp)], axis=0)
                y_diag = jnp.dot(mcat, xblk, preferred_element_type=F32)
                st = st_ref[j]
                y_off = jnp.dot(cg, st.astype(BF16), preferred_element_type=F32) * dout_full[:, j * CHUNK:(j + 1) * CHUNK]
                contrib = jnp.dot(bg_t, xdd[:, j * CHUNK:(j + 1) * CHUNK], preferred_element_type=F32)
                st_ref[j] = st * etot_full[:, j * CHUNK:(j + 1) * CHUNK] + contrib
                ys.append(y_diag + y_off)
        return s, xcv, jnp.concatenate(ys, axis=1)

    def fwd_chunk(c, carry):
        s, _, y = scan_chunk(c, True)
        yacc_ref[pl.ds(s, CHUNK), :] = y
        return carry

    def bwd_chunk(i, carry):
        c = n_chunks - 1 - i
        s, xcv, y = scan_chunk(c, False)
        xs = xcv[:, :SSD_INNER].astype(F32)
        y = yacc_ref[pl.ds(s, CHUNK), :] + y + dskip_ref[...] * xs
        y = y * _silu(z_ref[pl.ds(s, CHUNK), :].astype(F32))
        y_ref[pl.ds(s, CHUNK), :] = (_rms(y) * gain_ref[...]).astype(y_ref.dtype)
        return carry

    def store_state(out_ref):
        for j in range(SSD_HEADS // 2):
            sq = jnp.concatenate([st_ref[j], jnp.zeros((2 * SSD_HEADDIM - D_STATE, 2 * SSD_HEADDIM), F32)], axis=0)
            out_ref[j] = sq.T[:, :D_STATE]

    if has_h0:
        st_ref[...] = h0f_ref[...]
    else:
        st_ref[...] = jnp.zeros_like(st_ref)
    lax.fori_loop(0, n_chunks, fwd_chunk, 0)
    store_state(hf_ref)
    if has_h0:
        st_ref[...] = h0b_ref[...]
    else:
        st_ref[...] = jnp.zeros_like(st_ref)
    lax.fori_loop(0, n_chunks, bwd_chunk, 0)
    store_state(hb_ref)


def ssd_mixer(main, small, h0f, h0b, lw, n_seq, seq_len, stack=None):
    t = n_seq * seq_len
    has_h0 = h0f is not None
    pairs = SSD_HEADS // 2
    st_shape = (pairs, D_STATE, 2 * SSD_HEADDIM)
    out_st_shape = (pairs, 2 * SSD_HEADDIM, D_STATE)
    st_spec = pl.BlockSpec((None,) + st_shape, lambda b: (b, 0, 0, 0))
    if has_h0:
        out_st_spec = pl.BlockSpec((None,) + out_st_shape, lambda b: (b, 0, 0, 0))
        out_st = jax.ShapeDtypeStruct((n_seq,) + out_st_shape, F32)
    else:
        depth = stack[0]
        out_st_spec = pl.BlockSpec((None, None) + out_st_shape, lambda b: (b, lw["layer"], 0, 0, 0))
        out_st = jax.ShapeDtypeStruct((n_seq, depth) + out_st_shape, F32)
    full = lambda shape: pl.BlockSpec(shape, lambda b: (0,) * len(shape))
    in_specs = [
        pl.BlockSpec((seq_len, M_Q), lambda b: (b, 0)),
        pl.BlockSpec((seq_len, LANES), lambda b: (b, 0)),
    ]
    args = [main, small]
    if has_h0:
        in_specs += [st_spec, st_spec]
        args += [h0f, h0b]
    k_head = (jnp.arange(2 * LANES, dtype=I32) % LANES)[None, :, None] - SSD_HEADS * jnp.arange(2, dtype=I32)[:, None, None]
    expand = (k_head == (jnp.arange(SSD_INNER, dtype=I32) // SSD_HEADDIM)[None, None, :]).astype(BF16)
    colsel = (k_head == (jnp.arange(SSD_HEADS * CHUNK, dtype=I32) // CHUNK)[None, None, :]).astype(BF16)
    in_specs += [full((8, XBC_DIM)), full((1, XBC_DIM)), full((1, LANES)), full((1, LANES)),
                 full((1, SSD_INNER)), full((1, SSD_INNER)), full((2, 2 * LANES, SSD_INNER)),
                 full((2, 2 * LANES, SSD_HEADS * CHUNK))]
    args += [lw["conv_w"], lw["conv_b"], lw["dt_bias"], lw["a_log"], lw["d_skip"], lw["ssd_norm"], expand, colsel]
    aliases = {}
    if not has_h0:
        aliases = {len(args): 1, len(args) + 1: 2}
        in_specs += [pl.BlockSpec(memory_space=pl.ANY), pl.BlockSpec(memory_space=pl.ANY)]
        args += [stack[1], stack[2]]
    return pl.pallas_call(
        functools.partial(_ssd_kernel, seq_len=seq_len, has_h0=has_h0),
        grid=(n_seq,),
        in_specs=in_specs,
        out_specs=[pl.BlockSpec((seq_len, SSD_INNER), lambda b: (b, 0)), out_st_spec, out_st_spec],
        out_shape=[jax.ShapeDtypeStruct((t, SSD_INNER), BF16), out_st, out_st],
        input_output_aliases=aliases,
        scratch_shapes=[
            pltpu.VMEM((seq_len, XBC_DIM), BF16),
            pltpu.VMEM((seq_len, LANES), F32),
            pltpu.VMEM((seq_len, SSD_INNER), F32),
            pltpu.VMEM(st_shape, F32),
        ],
        compiler_params=_cparams("parallel"),
        name="ssd_mixer",
    )(*args)


def _swap_pairs(x):
    n = x.shape[-1]
    lane = lax.broadcasted_iota(I32, x.shape, x.ndim - 1)
    return jnp.where(lane % 2 == 0, pltpu.roll(x, n - 1, x.ndim - 1), pltpu.roll(x, 1, x.ndim - 1))


def _kv_prep_kernel(*refs, seq_len, rope):
    if rope:
        (k_ref, v_ref, gain_ref, cos_ref, sin_ref, ck_ref, cv_ref, kt_ref, v4_ref) = refs
    else:
        (k_ref, v_ref, gain_ref, kn_ref, vout_ref, kt_ref, v4_ref) = refs[:3] + refs[-4:]
    kf = k_ref[...].astype(F32)
    parts = []
    for g in range(KV_HEADS):
        kh = kf[:, g * HEAD_DIM:(g + 1) * HEAD_DIM]
        parts.append(kh * lax.rsqrt(jnp.mean(kh * kh, axis=-1, keepdims=True) + EPS))
    kn = jnp.concatenate(parts, axis=1) * gain_ref[...]
    if rope:
        kn = kn * cos_ref[...] + _swap_pairs(kn) * sin_ref[...]
    else:
        kn_ref[...] = kn
        vout_ref[...] = v_ref[...].astype(F32)
    kt_ref[:, 0:seq_len] = kn.T.astype(BF16)
    def with_ones(vh):
        return jnp.concatenate([vh, jnp.ones_like(vh)], axis=1).astype(BF16)

    vf = v_ref[...]
    for g in range(KV_HEADS):
        v4_ref[g, 0:seq_len, :] = with_ones(vf[:, g * HEAD_DIM:(g + 1) * HEAD_DIM])
    if rope:
        past = ck_ref.shape[0]
        kt_ref[:, seq_len:seq_len + past] = ck_ref[...].T.astype(BF16)
        cvf = cv_ref[...]
        for g in range(KV_HEADS):
            v4_ref[g, seq_len:seq_len + past, :] = with_ones(cvf[:, g * HEAD_DIM:(g + 1) * HEAD_DIM])


def kv_prep(main, k_gain, n_seq, seq_len, rope_kv=None, cache=None, stack=None):
    rope = rope_kv is not None
    aliases = {}
    in_specs = [
        pl.BlockSpec((seq_len, KV_INNER), lambda b: (b, M_K // KV_INNER)),
        pl.BlockSpec((seq_len, KV_INNER), lambda b: (b, M_V // KV_INNER)),
        pl.BlockSpec((1, KV_INNER), lambda b: (0, 0)),
    ]
    args = [main, main, k_gain]
    n_keys = seq_len
    out_specs, out_shape = [], []
    if rope:
        cache_k, cache_v, layer = cache
        past = cache_k.shape[2]
        n_keys += past
        in_specs += [
            pl.BlockSpec((seq_len, KV_INNER), lambda b: (0, 0)),
            pl.BlockSpec((seq_len, KV_INNER), lambda b: (0, 0)),
            pl.BlockSpec((None, None, past, KV_INNER), lambda b: (b, layer, 0, 0)),
            pl.BlockSpec((None, None, past, KV_INNER), lambda b: (b, layer, 0, 0)),
        ]
        args += [rope_kv[0], rope_kv[1], cache_k, cache_v]
    else:
        layer, depth, k_prev, v_prev = stack
        aliases = {len(args): 0, len(args) + 1: 1}
        in_specs += [pl.BlockSpec(memory_space=pl.ANY), pl.BlockSpec(memory_space=pl.ANY)]
        args += [k_prev, v_prev]
        for _ in range(2):
            out_specs.append(pl.BlockSpec((None, None, seq_len, KV_INNER), lambda b: (b, layer, 0, 0)))
            out_shape.append(jax.ShapeDtypeStruct((n_seq, depth, seq_len, KV_INNER), F32))
    out_specs += [
        pl.BlockSpec((None, KV_INNER, n_keys), lambda b: (b, 0, 0)),
        pl.BlockSpec((None, KV_HEADS, n_keys, 2 * HEAD_DIM), lambda b: (b, 0, 0, 0)),
    ]
    out_shape += [
        jax.ShapeDtypeStruct((n_seq, KV_INNER, n_keys), BF16),
        jax.ShapeDtypeStruct((n_seq, KV_HEADS, n_keys, 2 * HEAD_DIM), BF16),
    ]
    return pl.pallas_call(
        functools.partial(_kv_prep_kernel, seq_len=seq_len, rope=rope),
        grid=(n_seq,),
        in_specs=in_specs,
        out_specs=out_specs,
        out_shape=out_shape,
        input_output_aliases=aliases,
        compiler_params=_cparams("parallel"),
        name="kv_prep",
    )(*args)


def _attn_kernel(*refs, rope, tq):
    if rope:
        qa_ref, qb_ref, gain_ref, cos_ref, sin_ref, kt_ref, v4_ref, o_ref = refs
    else:
        qa_ref, qb_ref, gain_ref, kt_ref, v4_ref, o_ref = refs
    rep = N_HEADS // KV_HEADS
    scale = HEAD_DIM ** -0.5
    slabs = ATTN_INNER // LANES // 2
    heads = []
    for j in range(N_HEADS // 2):
        q_ref = qa_ref if j < slabs else qb_ref
        qs = q_ref[:, (j % slabs) * LANES:(j % slabs + 1) * LANES].astype(F32)
        qg = qs * gain_ref[...]
        if rope:
            qg = qg * cos_ref[...] + _swap_pairs(qg) * sin_ref[...]
        for hh in range(2):
            raw = qs[:, hh * HEAD_DIM:(hh + 1) * HEAD_DIM]
            inv = lax.rsqrt(jnp.mean(raw * raw, axis=-1, keepdims=True) + EPS) * scale
            heads.append((qg[:, hh * HEAD_DIM:(hh + 1) * HEAD_DIM] * inv).astype(BF16))
    for g in range(KV_HEADS):
        qstack = jnp.concatenate(heads[g * rep:(g + 1) * rep], axis=0)
        kt = kt_ref[g * HEAD_DIM:(g + 1) * HEAD_DIM, :]
        s = jnp.dot(qstack, kt, preferred_element_type=F32)
        m = jnp.max(s, axis=-1, keepdims=True)
        p = jnp.exp(s - m).astype(BF16)
        ov = jnp.dot(p, v4_ref[g], preferred_element_type=F32)
        o = ov[:, :HEAD_DIM] / ov[:, HEAD_DIM:HEAD_DIM + 1]
        for r in range(rep):
            h = g * rep + r
            o_ref[:, h * HEAD_DIM:(h + 1) * HEAD_DIM] = o[r * tq:(r + 1) * tq, :].astype(o_ref.dtype)


def attention(main, q_gain, kt, v4, n_seq, seq_len, tq, rope_q=None):
    t = n_seq * seq_len
    nq = seq_len // tq
    n_keys = kt.shape[2]
    rope = rope_q is not None
    half = ATTN_INNER // 2
    in_specs = [
        pl.BlockSpec((tq, half), lambda b, i: (b * nq + i, M_Q // half)),
        pl.BlockSpec((tq, half), lambda b, i: (b * nq + i, M_Q // half + 1)),
        pl.BlockSpec((1, LANES), lambda b, i: (0, 0)),
    ]
    args = [main, main, q_gain]
    if rope:
        in_specs += [pl.BlockSpec((tq, LANES), lambda b, i: (i, 0)), pl.BlockSpec((tq, LANES), lambda b, i: (i, 0))]
        args += [rope_q[0], rope_q[1]]
    in_specs += [
        pl.BlockSpec((None, KV_INNER, n_keys), lambda b, i: (b, 0, 0)),
        pl.BlockSpec((None, KV_HEADS, n_keys, 2 * HEAD_DIM), lambda b, i: (b, 0, 0, 0)),
    ]
    args += [kt, v4]
    return pl.pallas_call(
        functools.partial(_attn_kernel, rope=rope, tq=tq),
        grid=(n_seq, nq),
        in_specs=in_specs,
        out_specs=pl.BlockSpec((tq, ATTN_INNER), lambda b, i: (b * nq + i, 0)),
        out_shape=jax.ShapeDtypeStruct((t, ATTN_INNER), BF16),
        compiler_params=_cparams("parallel", "arbitrary"),
        name="attention",
    )(*args)


def _pool_kernel(p_ref, w_ref, scale_ref, o_ref, *, seq_len):
    n_chunks = seq_len // CHUNK
    win = CHUNK + 2 * HALO
    t_loc = lax.broadcasted_iota(I32, (CHUNK, win), 0)
    r_loc = lax.broadcasted_iota(I32, (CHUNK, win), 1) - HALO
    t_col = lax.broadcasted_iota(I32, (CHUNK, 1), 0)

    def chunk(c, carry):
        s = pl.multiple_of(c * CHUNK, CHUNK)
        w = _chunk_window(p_ref, c, n_chunks, seq_len).astype(BF16)
        outs = []
        for g, width in enumerate(POOL_WINDOWS):
            half = width // 2
            d = r_loc - t_loc
            band = jnp.where((d >= -half) & (d < width - half), 1.0, 0.0).astype(BF16)
            wg = w[:, g * POOL_GROUP_DIM:(g + 1) * POOL_GROUP_DIM]
            sums = jnp.dot(band, wg, preferred_element_type=F32)
            tg = t_col + s
            cnt = (jnp.minimum(tg + (width - half), seq_len) - jnp.maximum(tg - half, 0)).astype(F32)
            cur = wg[HALO:HALO + CHUNK, :].astype(F32)
            diff = sums / cnt - cur
            outs.append(jnp.dot(diff.astype(BF16), w_ref[g], preferred_element_type=F32))
        o_ref[pl.ds(s, CHUNK), :] = (jnp.concatenate(outs, axis=1) * scale_ref[...]).astype(o_ref.dtype)
        return carry

    lax.fori_loop(0, n_chunks, chunk, 0)


def pool_mixer(main, pool_w, layer, pool_scale, n_seq, seq_len):
    t = n_seq * seq_len
    return pl.pallas_call(
        functools.partial(_pool_kernel, seq_len=seq_len),
        grid=(n_seq,),
        in_specs=[
            pl.BlockSpec((seq_len, POOL_INNER), lambda b: (b, M_PIN // POOL_INNER)),
            pl.BlockSpec((None, POOL_GROUPS, POOL_GROUP_DIM, POOL_GROUP_DIM), lambda b: (layer, 0, 0, 0)),
            pl.BlockSpec((1, POOL_INNER), lambda b: (0, 0)),
        ],
        out_specs=pl.BlockSpec((seq_len, POOL_INNER), lambda b: (b, 0)),
        out_shape=jax.ShapeDtypeStruct((t, POOL_INNER), BF16),
        compiler_params=_cparams("parallel"),
        name="pool_mixer",
    )(main, pool_w, pool_scale)


def _merge_kernel(ys_ref, ya_ref, yp_ref, g0a_ref, g0b_ref, g1a_ref, g1b_ref, g2a_ref, g2b_ref, ws_ref, wa_ref,
                  wp_ref, o_ref):
    def gate(a_ref, b_ref):
        return jax.nn.sigmoid(jnp.concatenate([a_ref[...], b_ref[...]], axis=1).astype(F32))

    acc = gate(g0a_ref, g0b_ref) * jnp.dot(ys_ref[...], ws_ref[...], preferred_element_type=F32)
    acc = acc + gate(g1a_ref, g1b_ref) * jnp.dot(ya_ref[...], wa_ref[...], preferred_element_type=F32)
    acc = acc + gate(g2a_ref, g2b_ref) * jnp.dot(yp_ref[...], wp_ref[...], preferred_element_type=F32)
    o_ref[...] = acc.astype(o_ref.dtype)


def merge_branches(y_ssd, o_attn, o_pool, main, lw, tm=512):
    t = y_ssd.shape[0]
    half = D_MODEL // 2
    act = lambda: pl.BlockSpec((tm, SSD_INNER), lambda i: (i, 0))
    gate = lambda k: pl.BlockSpec((tm, half), lambda i: (i, M_GATE // half + k))
    wspec = lambda: pl.BlockSpec((None, SSD_INNER, D_MODEL), lambda i: (lw["layer"], 0, 0))
    return pl.pallas_call(
        _merge_kernel,
        grid=(t // tm,),
        in_specs=[act(), act(), act()] + [gate(k) for k in range(6)] + [wspec(), wspec(), wspec()],
        out_specs=pl.BlockSpec((tm, D_MODEL), lambda i: (i, 0)),
        out_shape=jax.ShapeDtypeStruct((t, D_MODEL), BF16),
        compiler_params=_cparams("parallel"),
        name="merge_branches",
    )(y_ssd, o_attn, o_pool, *([main] * 6), lw["w_ssd_out"], lw["w_attn_out"], lw["w_pool_out"])


def _route(logits_t, bias_col):
    scores = jax.nn.sigmoid(logits_t)
    biased = scores + bias_col
    sc = [scores[e:e + 1, :] for e in range(N_EXPERTS)]
    bi = [biased[e:e + 1, :] for e in range(N_EXPERTS)]
    group_scores = []
    for g in range(N_EXPERT_GROUPS):
        a, b, c, d = bi[4 * g:4 * g + 4]
        hi1, lo1 = jnp.maximum(a, b), jnp.minimum(a, b)
        hi2, lo2 = jnp.maximum(c, d), jnp.minimum(c, d)
        top1 = jnp.maximum(hi1, hi2)
        top2 = jnp.maximum(jnp.minimum(hi1, hi2), jnp.maximum(lo1, lo2))
        group_scores.append(top1 + top2)
    best = jnp.zeros_like(group_scores[0], dtype=I32)
    best_v = group_scores[0]
    for g in range(1, N_EXPERT_GROUPS):
        upd = group_scores[g] > best_v
        best = jnp.where(upd, g, best)
        best_v = jnp.where(upd, group_scores[g], best_v)
    vb, vs = [], []
    for j in range(EXPERTS_PER_GROUP):
        b_j, s_j = bi[j], sc[j]
        for g in range(1, N_EXPERT_GROUPS):
            sel = best == g
            b_j = jnp.where(sel, bi[4 * g + j], b_j)
            s_j = jnp.where(sel, sc[4 * g + j], s_j)
        vb.append(b_j)
        vs.append(s_j)

    def first_argmax(vals, excluded=None):
        idx = None
        for j in range(EXPERTS_PER_GROUP):
            v = vals[j] if excluded is None else jnp.where(excluded == j, -jnp.inf, vals[j])
            if idx is None:
                idx, cur = jnp.zeros_like(best), v
            else:
                upd = v > cur
                idx = jnp.where(upd, j, idx)
                cur = jnp.where(upd, v, cur)
        return idx

    i0 = first_argmax(vb)
    i1 = first_argmax(vb, excluded=i0)

    def pick(vals, idx):
        out = vals[0]
        for j in range(1, EXPERTS_PER_GROUP):
            out = jnp.where(idx == j, vals[j], out)
        return out

    s0, s1 = pick(vs, i0), pick(vs, i1)
    tot = s0 + s1
    ids = jnp.concatenate([best * EXPERTS_PER_GROUP + i0, best * EXPERTS_PER_GROUP + i1], axis=0)
    wts = jnp.concatenate([s0 / tot, s1 / tot], axis=0)
    return ids, wts


def _outproj_kernel(x_ref, m_ref, wo_ref, g1_ref, gain_ref, sh_ref, sc_ref, wr_ref, wr_hi_ref, br_ref,
                    xo_ref, h_ref, ids_ref, rank_ref, wcol_ref, cnt_ref, run_ref):
    tm = x_ref.shape[0]

    @pl.when(pl.program_id(0) == 0)
    def _():
        run_ref[...] = jnp.zeros_like(run_ref)

    mix = jnp.dot(m_ref[...], wo_ref[...], preferred_element_type=F32)
    xn = x_ref[...] + g1_ref[...] * mix
    xo_ref[...] = xn
    h = _rms(xn) * gain_ref[...] * (1.0 + sc_ref[...]) + sh_ref[...]
    h_hi, h_lo = _split_bf16(h)
    for j in range(ROW_TILES):
        h_ref[pl.ds(j, tm, stride=ROW_TILES), :] = h[:, j * LANES:(j + 1) * LANES]
    lg = jnp.dot(h_hi, wr_ref[...], preferred_element_type=F32) + jnp.dot(h_lo, wr_hi_ref[...],
                                                                         preferred_element_type=F32)
    lg_t = lg.T
    logits_t = lg_t[0:N_EXPERTS, :] + lg_t[N_EXPERTS:2 * N_EXPERTS, :]
    ids, wts = _route(logits_t, br_ref[...])

    def fold(rows):
        return jnp.concatenate([rows[k:k + 1, j * LANES:(j + 1) * LANES] for k in range(2)
                                for j in range(tm // LANES)], axis=0)

    ids_ref[...] = fold(ids)
    e_iota = lax.broadcasted_iota(I32, (N_EXPERTS, tm), 0)
    earlier = (lax.broadcasted_iota(I32, (tm, tm), 0) < lax.broadcasted_iota(I32, (tm, tm), 1))
    earlier = jnp.where(earlier, 1.0, 0.0).astype(BF16)
    oh0 = jnp.where(e_iota == ids[0:1, :], 1.0, 0.0)
    oh1 = jnp.where(e_iota == ids[1:2, :], 1.0, 0.0)
    before0 = jnp.dot(oh0.astype(BF16), earlier, preferred_element_type=F32)
    before1 = jnp.dot(oh1.astype(BF16), earlier, preferred_element_type=F32)
    c0 = jnp.sum(oh0, axis=1, keepdims=True)
    c1 = jnp.sum(oh1, axis=1, keepdims=True)
    running = run_ref[...]
    rank0 = jnp.sum(oh0 * (before0 + running), axis=0, keepdims=True)
    rank1 = jnp.sum(oh1 * (before1 + running + c0), axis=0, keepdims=True)
    rank_ref[...] = fold(jnp.concatenate([rank0, rank1], axis=0).astype(I32))
    running = running + c0 + c1
    run_ref[...] = running
    cnt_ref[...] = jnp.broadcast_to(running, cnt_ref.shape)
    r_iota = lax.broadcasted_iota(I32, (LANES, tm), 0)
    w_rows = jnp.where(r_iota == 0, wts[0:1, :], jnp.where(r_iota == 1, wts[1:2, :], 0.0))
    wcol_ref[...] = w_rows.T


def outproj_route(x, merged, mods, lw, gw, path):
    t = x.shape[0]
    tm = ROUTE_TILE
    rf = _row_fn(path, tm)
    row = lambda: pl.BlockSpec((1, D_MODEL), lambda i: (0, 0))
    return pl.pallas_call(
        _outproj_kernel,
        grid=(t // tm,),
        in_specs=[
            pl.BlockSpec((tm, D_MODEL), lambda i: (i, 0)),
            pl.BlockSpec((tm, D_MODEL), lambda i: (i, 0)),
            pl.BlockSpec((None, D_MODEL, D_MODEL), lambda i: (lw["layer"], 0, 0)),
            _mod_spec(2, rf),
            row(),
            _mod_spec(3, rf),
            _mod_spec(4, rf),
            pl.BlockSpec((D_MODEL, LANES), lambda i: (0, 0)),
            pl.BlockSpec((D_MODEL, LANES), lambda i: (0, 0)),
            pl.BlockSpec((N_EXPERTS, 1), lambda i: (0, 0)),
        ],
        out_specs=[
            pl.BlockSpec((tm, D_MODEL), lambda i: (i, 0)),
            pl.BlockSpec((tm * ROW_TILES, LANES), lambda i: (i, 0)),
            pl.BlockSpec((None, 2 * tm // LANES, LANES), lambda i: (i, 0, 0)),
            pl.BlockSpec((None, 2 * tm // LANES, LANES), lambda i: (i, 0, 0)),
            pl.BlockSpec((tm, LANES), lambda i: (i, 0)),
            pl.BlockSpec((N_EXPERTS, LANES), lambda i: (0, 0)),
        ],
        out_shape=[
            jax.ShapeDtypeStruct((t, D_MODEL), F32),
            jax.ShapeDtypeStruct((t * ROW_TILES, LANES), F32),
            jax.ShapeDtypeStruct((t // tm, 2 * tm // LANES, LANES), I32),
            jax.ShapeDtypeStruct((t // tm, 2 * tm // LANES, LANES), I32),
            jax.ShapeDtypeStruct((t, LANES), F32),
            jax.ShapeDtypeStruct((N_EXPERTS, LANES), F32),
        ],
        scratch_shapes=[pltpu.VMEM((N_EXPERTS, 1), F32)],
        compiler_params=_cparams("arbitrary"),
        name="outproj_route",
    )(x, merged, lw["w_out"], mods, lw["norm_ffn"], mods, mods, gw["w_router_cat"], gw["w_router_hi"],
      gw["b_router"])


def _moe_kernel(src_ref, n_used_ref, first_ref, expert_ref, next_ref, wslot_ref, h_hbm, wg_hbm, wu_hbm, wd_hbm,
                o_ref, xbuf0, xbuf1, xbuf2, wg_buf, wu_buf, wd_buf, sem, wsem, *, layer):
    i = pl.program_id(0)
    n_used = n_used_ref[0]
    bufs = (xbuf0, xbuf1, xbuf2)

    def weight_copies(expert, ws):
        return [pltpu.make_async_copy(hbm.at[layer, expert], buf.at[ws], wsem.at[ws])
                for hbm, buf in ((wg_hbm, wg_buf), (wu_hbm, wu_buf), (wd_hbm, wd_buf))]

    @pl.when(i == 0)
    def _():
        for cp in weight_copies(expert_ref[0], 0):
            cp.start()

    @pl.when((i < n_used) & (first_ref[i] == 1))
    def _():
        ws = wslot_ref[i]
        for cp in weight_copies(expert_ref[i], ws):
            cp.wait()

        @pl.when(next_ref[i] >= 0)
        def _():
            for cp in weight_copies(next_ref[i], 1 - ws):
                cp.start()

    def start_tile(tile, slot):
        for r in range(MOE_TILE):
            row = pl.multiple_of(src_ref[tile * MOE_TILE + r] * ROW_TILES, ROW_TILES)
            pltpu.make_async_copy(h_hbm.at[pl.ds(row, ROW_TILES)], bufs[slot].at[pl.ds(r * ROW_TILES, ROW_TILES)],
                                  sem.at[slot]).start(priority=r % 2)

    def wait_tile(slot):
        pltpu.make_async_copy(h_hbm.at[pl.ds(0, MOE_TILE * ROW_TILES)], bufs[slot], sem.at[slot]).wait()

    def compute(slot):
        ws = wslot_ref[i]
        x = jnp.concatenate([bufs[slot][pl.ds(j, MOE_TILE, stride=ROW_TILES), :].astype(BF16)
                             for j in range(ROW_TILES)], axis=1)
        a = jnp.dot(x, wg_buf[ws], preferred_element_type=F32)
        u = jnp.dot(x, wu_buf[ws], preferred_element_type=F32)
        act = _silu(a) * u
        o_ref[...] = jnp.dot(act.astype(BF16), wd_buf[ws], preferred_element_type=F32).astype(o_ref.dtype)

    @pl.when(i == 0)
    def _():
        for t0 in range(MOE_AHEAD):
            @pl.when(t0 < n_used)
            def _(t0=t0):
                start_tile(t0, t0)

    for slot in range(MOE_BUFS):
        @pl.when((i % MOE_BUFS == slot) & (i + MOE_AHEAD < n_used))
        def _(slot=slot):
            wait_tile(slot)
            start_tile(i + MOE_AHEAD, (slot + MOE_AHEAD) % MOE_BUFS)
            compute(slot)

        @pl.when((i % MOE_BUFS == slot) & (i + MOE_AHEAD >= n_used) & (i < n_used))
        def _(slot=slot):
            wait_tile(slot)
            compute(slot)

    @pl.when(i >= n_used)
    def _():
        o_ref[...] = jnp.zeros_like(o_ref)


def moe_experts(h, src, tile_expert, n_used, lw):
    rows = src.shape[0]
    n_tiles = rows // MOE_TILE
    t_idx = jnp.arange(n_tiles, dtype=I32)
    used = t_idx < n_used[0]
    prev = jnp.concatenate([jnp.full((1,), -1, I32), tile_expert[:-1]])
    first = used & (tile_expert != prev)
    wslot = (jnp.cumsum(first.astype(I32)) - 1) % 2
    first_idx = jnp.where(first, t_idx, n_tiles)
    later_first = jnp.concatenate([lax.cummin(first_idx[::-1])[::-1][1:], jnp.full((1,), n_tiles, I32)])
    next_expert = jnp.where(later_first < n_tiles, tile_expert[jnp.minimum(later_first, n_tiles - 1)], -1)
    grid_spec = pltpu.PrefetchScalarGridSpec(
        num_scalar_prefetch=6,
        grid=(n_tiles,),
        in_specs=[pl.BlockSpec(memory_space=pl.ANY)] * 4,
        out_specs=pl.BlockSpec((MOE_TILE, D_MODEL), lambda i, *_: (i, 0)),
        scratch_shapes=[pltpu.VMEM((MOE_TILE * ROW_TILES, LANES), F32) for _ in range(MOE_BUFS)]
        + [pltpu.VMEM((2, D_MODEL, D_FF), BF16), pltpu.VMEM((2, D_MODEL, D_FF), BF16),
           pltpu.VMEM((2, D_FF, D_MODEL), BF16), pltpu.SemaphoreType.DMA((MOE_BUFS,)), pltpu.SemaphoreType.DMA((2,))],
    )
    return pl.pallas_call(
        functools.partial(_moe_kernel, layer=lw["layer"]),
        grid_spec=grid_spec,
        out_shape=jax.ShapeDtypeStruct((rows, D_MODEL), BF16),
        compiler_params=_cparams("arbitrary"),
        name="moe_experts",
    )(src, n_used, first.astype(I32), tile_expert, next_expert.astype(I32), wslot.astype(I32), h,
      lw["w_gate_ff"], lw["w_up_ff"], lw["w_down_ff"])


def _moe_plan(ids, rank, counts):
    two_t = ids.size
    ids = ids.reshape(two_t)
    rank = rank.reshape(two_t)
    counts = counts[:, 0].astype(I32)
    padded = ((counts + MOE_TILE - 1) // MOE_TILE) * MOE_TILE
    ends = jnp.cumsum(padded)
    starts = ends - padded
    onehot = ids[:, None] == jnp.arange(N_EXPERTS, dtype=I32)[None, :]
    pos = jnp.sum(jnp.where(onehot, starts[None, :], 0), axis=1) + rank
    n_rows = two_t + N_EXPERTS * MOE_TILE
    n_tiles = n_rows // MOE_TILE
    pair = jnp.arange(two_t, dtype=I32)
    token = (pair // (2 * ROUTE_TILE)) * ROUTE_TILE + pair % ROUTE_TILE
    src = jnp.zeros((n_rows,), I32).at[pos].set(token, mode="promise_in_bounds", unique_indices=True)
    tile_start = jnp.arange(n_tiles, dtype=I32) * MOE_TILE
    tile_expert = jnp.minimum(jnp.sum((tile_start[:, None] >= ends[None, :]).astype(I32), axis=1), N_EXPERTS - 1)
    n_used = (ends[-1] // MOE_TILE).reshape(1).astype(I32)
    return pos, src, tile_expert.astype(I32), n_used


def moe_ffn(h2, ids, rank, counts, lw):
    pos, src, tile_expert, n_used = _moe_plan(ids, rank, counts)
    ys = moe_experts(h2, src, tile_expert, n_used, lw)
    return ys.at[pos].get(mode="promise_in_bounds", unique_indices=True)


def _rope_tables(n_tokens):
    rows = n_tokens // GRID_W
    row = jnp.repeat(jnp.arange(rows, dtype=F32), GRID_W)
    col = jnp.tile(jnp.arange(GRID_W, dtype=F32), rows)
    n_freq = HEAD_DIM // 4
    inv = ROPE_THETA ** (-jnp.arange(n_freq, dtype=F32) / n_freq)
    ang = jnp.concatenate([row[:, None] * inv, col[:, None] * inv], axis=-1)
    cos = jnp.repeat(jnp.cos(ang), 2, axis=-1)
    sin = jnp.repeat(jnp.sin(ang), 2, axis=-1)
    sign = jnp.tile(jnp.array([-1.0, 1.0], F32), HEAD_DIM // 2)
    sin = sin * sign
    return cos, sin


def _prep_stacked_weights(p):
    w_in = p["w_in"]
    depth = w_in.shape[0]
    w_main = jnp.concatenate([w_in[:, :, :_O_DTF], w_in[:, :, _O_Q:]], axis=2).astype(BF16)
    w_small = jnp.concatenate([w_in[:, :, _O_DTF:_O_Q],
                               jnp.zeros((depth, D_MODEL, LANES - 2 * SSD_HEADS), F32)], axis=2).astype(BF16)
    out = {"w_main": w_main, "w_small": w_small}
    for name in ("pool_w", "w_ssd_out", "w_attn_out", "w_pool_out", "w_out", "w_gate_ff", "w_up_ff", "w_down_ff"):
        out[name] = p[name].astype(BF16)
    return out


def _prep_layer_weights(l, p, stacked):
    pad_lanes = lambda v: jnp.concatenate([v.reshape(-1), jnp.zeros((LANES - v.size,), F32)]).reshape(1, LANES)
    return {
        **stacked,
        "layer": l,
        "conv_w": jnp.concatenate([p["conv_w"][l], jnp.zeros((8 - CONV_W, XBC_DIM), F32)], axis=0),
        "conv_b": p["conv_b"][l].reshape(1, XBC_DIM),
        "dt_bias": pad_lanes(p["dt_bias"][l]),
        "a_log": pad_lanes(p["a_log"][l]),
        "d_skip": jnp.repeat(p["d_skip"][l], SSD_HEADDIM).reshape(1, SSD_INNER),
        "ssd_norm": p["ssd_norm"][l].reshape(1, SSD_INNER),
        "q_gain": jnp.tile(p["q_norm"][l], LANES // HEAD_DIM).reshape(1, LANES),
        "k_gain": jnp.tile(p["k_norm"][l], KV_HEADS).reshape(1, KV_INNER),
        "pool_scale": p["pool_scale"][l].reshape(1, POOL_INNER),
        "norm_mix": p["norm_mix"][l].reshape(1, D_MODEL),
        "norm_ffn": p["norm_ffn"][l].reshape(1, D_MODEL),
    }


def _mixer_stage(h, x, mods, lw, gw, path, n_seq, seq_len, tq, h0, rope, cache, outs):
    t = n_seq * seq_len
    main = matmul(h, lw["w_main"], lw["layer"], BF16, min(t, 2048), MAIN_TN, "in_proj_main")
    small = matmul(h, lw["w_small"], lw["layer"], F32, min(t, 2048), LANES, "in_proj_dt")
    if rope is None:
        depth = outs["k"].shape[1]
        y_ssd, hf, hb = ssd_mixer(main, small, None, None, lw, n_seq, seq_len, stack=(depth, outs["hf"], outs["hb"]))
        k_st, v_st, kt, v4 = kv_prep(main, lw["k_gain"], n_seq, seq_len,
                                     stack=(lw["layer"], depth, outs["k"], outs["v"]))
        outs = {"k": k_st, "v": v_st, "hf": hf, "hb": hb}
        o_attn = attention(main, lw["q_gain"], kt, v4, n_seq, seq_len, tq)
    else:
        y_ssd, _, _ = ssd_mixer(main, small, h0[0], h0[1], lw, n_seq, seq_len)
        kt, v4 = kv_prep(main, lw["k_gain"], n_seq, seq_len, rope_kv=rope[0], cache=cache)
        o_attn = attention(main, lw["q_gain"], kt, v4, n_seq, seq_len, tq, rope_q=rope[1])
    o_pool = pool_mixer(main, lw["pool_w"], lw["layer"], lw["pool_scale"], n_seq, seq_len)
    merged = merge_branches(y_ssd, o_attn, o_pool, main, lw)
    x_new, h2, ids, rank, wcol, counts = outproj_route(x, merged, mods, lw, gw, path)
    y2 = moe_ffn(h2, ids, rank, counts, lw)
    return x_new, y2, wcol, outs


def kernel(x_prompt, x_sample, cache_k, cache_v, state_ssd_fwd, state_ssd_bwd, c, c_ctx, w_ada, b_ada, norm_mix,
           norm_ffn, w_in, conv_w, conv_b, a_log, dt_bias, d_skip, ssd_norm, q_norm, k_norm, pool_w, pool_scale,
           w_ssd_out, w_attn_out, w_pool_out, w_out, w_router, b_router, w_gate_ff, w_up_ff, w_down_ff, norm_final):
    p = dict(norm_mix=norm_mix, norm_ffn=norm_ffn, w_in=w_in, conv_w=conv_w, conv_b=conv_b, a_log=a_log,
             dt_bias=dt_bias, d_skip=d_skip, ssd_norm=ssd_norm, q_norm=q_norm, k_norm=k_norm, pool_w=pool_w,
             pool_scale=pool_scale, w_ssd_out=w_ssd_out, w_attn_out=w_attn_out, w_pool_out=w_pool_out, w_out=w_out,
             w_gate_ff=w_gate_ff, w_up_ff=w_up_ff, w_down_ff=w_down_ff)
    bp, lc, _ = x_prompt.shape
    depth = w_in.shape[0]
    bs, ls, _ = x_sample.shape
    past = cache_k.shape[2]
    tc, tl = bp * lc, bs * ls

    mod_rows = -(-(1 + bs) // 8) * 8
    cond = jnp.concatenate([c_ctx[None, :], c, jnp.zeros((mod_rows - 1 - bs, D_MODEL), F32)], axis=0)
    mods_all = ada_mods(cond, w_ada, b_ada).reshape(depth, mod_rows, 6, 1, D_MODEL)

    wr_hi, wr_lo = _split_bf16(w_router)
    zpad = jnp.zeros((D_MODEL, LANES - 2 * N_EXPERTS), BF16)
    gw = {
        "w_router_cat": jnp.concatenate([wr_hi, wr_lo, zpad], axis=1),
        "w_router_hi": jnp.concatenate([wr_hi, jnp.zeros((D_MODEL, LANES - N_EXPERTS), BF16)], axis=1),
        "b_router": b_router.reshape(N_EXPERTS, 1),
    }
    cos, sin = _rope_tables(ls)
    rope = ((jnp.tile(cos, (1, KV_HEADS)), jnp.tile(sin, (1, KV_HEADS))),
            (jnp.tile(cos, (1, LANES // HEAD_DIM)), jnp.tile(sin, (1, LANES // HEAD_DIM))))
    cache_k4 = cache_k.reshape(bs, depth, past, KV_INNER)
    cache_v4 = cache_v.reshape(bs, depth, past, KV_INNER)
    pairs = SSD_HEADS // 2

    def pack_state(st):
        st = st.reshape(bs, depth, pairs, 2, SSD_HEADDIM, D_STATE)
        return st.transpose(0, 1, 2, 5, 3, 4).reshape(bs, depth, pairs, D_STATE, 2 * SSD_HEADDIM)

    st_f = pack_state(state_ssd_fwd)
    st_b = pack_state(state_ssd_bwd)

    path_c = (0, False, lc)
    path_l = (1, True, ls)
    xc = x_prompt.reshape(tc, D_MODEL)
    xl = x_sample.reshape(tl, D_MODEL)
    outs = {"k": jnp.zeros((bp, depth, lc, KV_INNER), F32), "v": jnp.zeros((bp, depth, lc, KV_INNER), F32),
            "hf": jnp.zeros((bp, depth, pairs, 2 * SSD_HEADDIM, D_STATE), F32),
            "hb": jnp.zeros((bp, depth, pairs, 2 * SSD_HEADDIM, D_STATE), F32)}
    hc = hl = None
    y2c = y2l = wc = wl = None
    stacked = _prep_stacked_weights(p)
    for l in range(depth):
        lw = _prep_layer_weights(l, p, stacked)
        mods = mods_all[l]
        if l == 0:
            hc = modulate_first(xc, mods, lw["norm_mix"], path_c)
            hl = modulate_first(xl, mods, lw["norm_mix"], path_l)
        else:
            xc, hc = residual_modulate(xc, y2c, wc, mods_all[l - 1], mods, lw["norm_mix"], path_c)
            xl, hl = residual_modulate(xl, y2l, wl, mods_all[l - 1], mods, lw["norm_mix"], path_l)
        xc, y2c, wc, outs = _mixer_stage(hc, xc, mods, lw, gw, path_c, bp, lc, lc, (None, None), None, None, outs)
        xl, y2l, wl, _ = _mixer_stage(
            hl, xl, mods, lw, gw, path_l, bs, ls, CHUNK, (st_f[:, l], st_b[:, l]), rope, (cache_k4, cache_v4, l), None)
    gain_f = norm_final.reshape(1, D_MODEL)
    y_prompt = residual_final(xc, y2c, wc, mods_all[depth - 1], gain_f, path_c).reshape(bp, lc, D_MODEL)
    y_sample = residual_final(xl, y2l, wl, mods_all[depth - 1], gain_f, path_l).reshape(bs, ls, D_MODEL)
    return (y_prompt, y_sample, outs["k"].reshape(bp, depth, lc, KV_HEADS, HEAD_DIM),
            outs["v"].reshape(bp, depth, lc, KV_HEADS, HEAD_DIM),
            outs["hf"].reshape(bp, depth, SSD_HEADS, SSD_HEADDIM, D_STATE),
            outs["hb"].reshape(bp, depth, SSD_HEADS, SSD_HEADDIM, D_STATE))
```

```python
import functools

import jax
import jax.numpy as jnp
from jax import lax
from jax.experimental import pallas as pl
from jax.experimental.pallas import tpu as pltpu

F32 = jnp.float32
BF16 = jnp.bfloat16
I32 = jnp.int32

D_MODEL = 2048
GRID_W = 64
EPS = 1e-6

SSD_HEADS = 16
SSD_HEADDIM = 64
SSD_INNER = SSD_HEADS * SSD_HEADDIM
SSD_GROUPS = 4
D_STATE = 64
CONV_W = 5
CHUNK = 128
XBC_DIM = SSD_INNER + 2 * SSD_GROUPS * D_STATE

N_HEADS = 16
KV_HEADS = 4
HEAD_DIM = 64
ATTN_INNER = N_HEADS * HEAD_DIM
KV_INNER = KV_HEADS * HEAD_DIM
ROPE_THETA = 10000.0

POOL_GROUPS = 4
POOL_INNER = 1024
POOL_GROUP_DIM = POOL_INNER // POOL_GROUPS
POOL_WINDOWS = (2, 4, 8, 16)

N_EXPERTS = 16
N_EXPERT_GROUPS = 4
EXPERTS_PER_GROUP = N_EXPERTS // N_EXPERT_GROUPS
D_FF = 512

_O_Z = 0
_O_XBC = _O_Z + SSD_INNER
_O_DTF = _O_XBC + XBC_DIM
_O_DTB = _O_DTF + SSD_HEADS
_O_Q = _O_DTB + SSD_HEADS
_O_K = _O_Q + ATTN_INNER
_O_V = _O_K + KV_INNER
_O_PIN = _O_V + KV_INNER
_O_GATE = _O_PIN + POOL_INNER

M_Z = 0
M_XBC = M_Z + SSD_INNER
M_Q = M_XBC + XBC_DIM
M_K = M_Q + ATTN_INNER
M_V = M_K + KV_INNER
M_PIN = M_V + KV_INNER
M_GATE = M_PIN + POOL_INNER
MAIN_N = M_GATE + 3 * D_MODEL
MAIN_TN = 1024
LANES = 128

HALO = 16
MOE_TILE = 256
MOE_AHEAD = 1
MOE_BUFS = MOE_AHEAD + 1
ROUTE_TILE = 512
ROW_TILES = D_MODEL // LANES
VMEM_LIMIT = 56 * 1024 * 1024


def _cparams(*sem):
    return pltpu.CompilerParams(dimension_semantics=sem, vmem_limit_bytes=VMEM_LIMIT)


def _silu(x):
    return x * jax.nn.sigmoid(x)


def _rms(xf):
    return xf * lax.rsqrt(jnp.mean(xf * xf, axis=-1, keepdims=True) + EPS)


def _split_bf16(x):
    hi = x.astype(BF16)
    lo = (x - hi.astype(F32)).astype(BF16)
    return hi, lo


def _ada_kernel(cond_ref, w_ref, b_ref, o_ref):
    c = _silu(cond_ref[...])
    o_ref[...] = jnp.dot(c.astype(BF16), w_ref[...].astype(BF16), preferred_element_type=F32) + b_ref[...]


def ada_mods(cond, w_ada, b_ada, tn=1024):
    rows = cond.shape[0]
    depth, _, n = w_ada.shape
    return pl.pallas_call(
        _ada_kernel,
        grid=(depth, n // tn),
        in_specs=[
            pl.BlockSpec((rows, D_MODEL), lambda l, j: (0, 0)),
            pl.BlockSpec((None, D_MODEL, tn), lambda l, j: (l, 0, j)),
            pl.BlockSpec((None, 1, tn), lambda l, j: (l, 0, j)),
        ],
        out_specs=pl.BlockSpec((None, rows, tn), lambda l, j: (l, 0, j)),
        out_shape=jax.ShapeDtypeStruct((depth, rows, n), F32),
        compiler_params=_cparams("parallel", "parallel"),
        name="ada_mods",
    )(cond, w_ada, b_ada.reshape(depth, 1, n))


def _mod_spec(which, row_of_tile):
    return pl.BlockSpec((None, None, 1, D_MODEL), lambda i: (row_of_tile(i), which, 0, 0))


def _row_fn(path, tm):
    row0, per_seq, seq_len = path
    if per_seq:
        return lambda i: row0 + (i * tm) // seq_len
    return lambda i: row0


def _modulate_kernel(x_ref, g_ref, sh_ref, sc_ref, h_ref):
    xf = x_ref[...]
    h_ref[...] = (_rms(xf) * g_ref[...] * (1.0 + sc_ref[...]) + sh_ref[...]).astype(h_ref.dtype)


def modulate_first(x, mods, gain, path, tm=512):
    t = x.shape[0]
    rf = _row_fn(path, tm)
    return pl.pallas_call(
        _modulate_kernel,
        grid=(t // tm,),
        in_specs=[
            pl.BlockSpec((tm, D_MODEL), lambda i: (i, 0)),
            pl.BlockSpec((1, D_MODEL), lambda i: (0, 0)),
            _mod_spec(0, rf),
            _mod_spec(1, rf),
        ],
        out_specs=pl.BlockSpec((tm, D_MODEL), lambda i: (i, 0)),
        out_shape=jax.ShapeDtypeStruct((t, D_MODEL), BF16),
        compiler_params=_cparams("parallel"),
        name="modulate_first",
    )(x, gain, mods, mods)


def _moe_residual(x_ref, ya_ref, yb_ref, w_ref, g2_ref):
    w = w_ref[...]
    y = w[:, 0:1] * ya_ref[...].astype(F32) + w[:, 1:2] * yb_ref[...].astype(F32)
    return x_ref[...] + g2_ref[...] * y


def _residual_modulate_kernel(x_ref, ya_ref, yb_ref, w_ref, g2_ref, gain_ref, sh_ref, sc_ref, xo_ref, h_ref):
    xn = _moe_residual(x_ref, ya_ref, yb_ref, w_ref, g2_ref)
    xo_ref[...] = xn
    h_ref[...] = (_rms(xn) * gain_ref[...] * (1.0 + sc_ref[...]) + sh_ref[...]).astype(h_ref.dtype)


def residual_modulate(x, y2, wcol, mods_prev, mods_next, gain_next, path):
    t = x.shape[0]
    tm = ROUTE_TILE
    nt = t // tm
    rf = _row_fn(path, tm)
    return pl.pallas_call(
        _residual_modulate_kernel,
        grid=(nt,),
        in_specs=[
            pl.BlockSpec((tm, D_MODEL), lambda i: (i, 0)),
            pl.BlockSpec((tm, D_MODEL), lambda i: (2 * i, 0)),
            pl.BlockSpec((tm, D_MODEL), lambda i: (2 * i + 1, 0)),
            pl.BlockSpec((tm, LANES), lambda i: (i, 0)),
            _mod_spec(5, rf),
            pl.BlockSpec((1, D_MODEL), lambda i: (0, 0)),
            _mod_spec(0, rf),
            _mod_spec(1, rf),
        ],
        out_specs=[
            pl.BlockSpec((tm, D_MODEL), lambda i: (i, 0)),
            pl.BlockSpec((tm, D_MODEL), lambda i: (i, 0)),
        ],
        out_shape=[
            jax.ShapeDtypeStruct((t, D_MODEL), F32),
            jax.ShapeDtypeStruct((t, D_MODEL), BF16),
        ],
        compiler_params=_cparams("parallel"),
        name="residual_modulate",
    )(x, y2, y2, wcol, mods_prev, gain_next, mods_next, mods_next)


def _residual_final_kernel(x_ref, ya_ref, yb_ref, w_ref, g2_ref, gain_ref, y_ref):
    xn = _moe_residual(x_ref, ya_ref, yb_ref, w_ref, g2_ref)
    y_ref[...] = _rms(xn) * gain_ref[...]


def residual_final(x, y2, wcol, mods_prev, gain, path):
    t = x.shape[0]
    tm = ROUTE_TILE
    nt = t // tm
    rf = _row_fn(path, tm)
    return pl.pallas_call(
        _residual_final_kernel,
        grid=(nt,),
        in_specs=[
            pl.BlockSpec((tm, D_MODEL), lambda i: (i, 0)),
            pl.BlockSpec((tm, D_MODEL), lambda i: (2 * i, 0)),
            pl.BlockSpec((tm, D_MODEL), lambda i: (2 * i + 1, 0)),
            pl.BlockSpec((tm, LANES), lambda i: (i, 0)),
            _mod_spec(5, rf),
            pl.BlockSpec((1, D_MODEL), lambda i: (0, 0)),
        ],
        out_specs=pl.BlockSpec((tm, D_MODEL), lambda i: (i, 0)),
        out_shape=jax.ShapeDtypeStruct((t, D_MODEL), F32),
        compiler_params=_cparams("parallel"),
        name="residual_final",
    )(x, y2, y2, wcol, mods_prev, gain)


def _mm_kernel(a_ref, b_ref, o_ref):
    o_ref[...] = jnp.dot(a_ref[...], b_ref[...], preferred_element_type=F32).astype(o_ref.dtype)


def matmul(a, b, layer, out_dtype, tm, tn, name):
    m, k = a.shape
    n = b.shape[2]
    return pl.pallas_call(
        _mm_kernel,
        grid=(m // tm, n // tn),
        in_specs=[
            pl.BlockSpec((tm, k), lambda i, j: (i, 0)),
            pl.BlockSpec((None, k, tn), lambda i, j: (layer, 0, j)),
        ],
        out_specs=pl.BlockSpec((tm, tn), lambda i, j: (i, j)),
        out_shape=jax.ShapeDtypeStruct((m, n), out_dtype),
        compiler_params=_cparams("parallel", "arbitrary"),
        name=name,
    )(a, b)


def _chunk_window(ref, c, n_chunks, seq_len):
    s = pl.multiple_of(c * CHUNK, CHUNK)
    cur = ref[pl.ds(s, CHUNK), :].astype(F32)
    sp = pl.multiple_of(jnp.maximum(s - HALO, 0), HALO)
    sn = pl.multiple_of(jnp.minimum(s + CHUNK, seq_len - HALO), HALO)
    prev = ref[pl.ds(sp, HALO), :].astype(F32)
    nxt = ref[pl.ds(sn, HALO), :].astype(F32)
    prev = jnp.where(c > 0, prev, 0.0)
    nxt = jnp.where(c < n_chunks - 1, nxt, 0.0)
    return jnp.concatenate([prev, cur, nxt], axis=0)


def _ssd_kernel(*refs, seq_len, has_h0):
    if has_h0:
        (zx_ref, dt_ref, h0f_ref, h0b_ref, cw_ref, cb_ref, dtb_ref, alog_ref, dskip_ref, gain_ref,
         expand_ref, colsel_ref, y_ref, hf_ref, hb_ref, xc_ref, sp_ref, yacc_ref, st_ref) = refs
    else:
        (zx_ref, dt_ref, cw_ref, cb_ref, dtb_ref, alog_ref, dskip_ref, gain_ref,
         expand_ref, colsel_ref, _, _, y_ref, hf_ref, hb_ref, xc_ref, sp_ref, yacc_ref, st_ref) = refs
        h0f_ref = h0b_ref = None
    z_ref = zx_ref.at[:, pl.ds(M_Z, SSD_INNER)]
    xbc_ref = zx_ref.at[:, pl.ds(M_XBC, XBC_DIM)]
    n_chunks = seq_len // CHUNK
    win = CHUNK + 2 * HALO

    def conv_chunk(c, carry):
        w = _chunk_window(xbc_ref, c, n_chunks, seq_len)
        acc = jnp.zeros((CHUNK, XBC_DIM), F32) + cb_ref[...]
        for j in range(CONV_W):
            shift = (CONV_W // 2 - j) % win
            wj = w if shift == 0 else pltpu.roll(w, shift, 0)
            acc = acc + cw_ref[j:j + 1, :] * wj[HALO:HALO + CHUNK, :]
        s = pl.multiple_of(c * CHUNK, CHUNK)
        xc_ref[pl.ds(s, CHUNK), :] = _silu(acc).astype(BF16)
        return carry

    lax.fori_loop(0, n_chunks, conv_chunk, 0)

    xdt = dt_ref[...] + dtb_ref[...]
    sp_ref[...] = jnp.maximum(xdt, 0.0) + jnp.log1p(jnp.exp(-jnp.abs(xdt)))
    a_row = -jnp.exp(alog_ref[...])

    rows = lax.broadcasted_iota(I32, (CHUNK, CHUNK), 0)
    cols = lax.broadcasted_iota(I32, (CHUNK, CHUNK), 1)
    lane_lo = cols < SSD_HEADDIM

    def cat2(x):
        hi, lo = _split_bf16(x)
        return jnp.concatenate([hi, lo], axis=1)

    def scan_chunk(c, forward):
        d = 0 if forward else 1
        off = d * SSD_HEADS
        s = pl.multiple_of(c * CHUNK, CHUNK)
        xcv = xc_ref[pl.ds(s, CHUNK), :]
        dtc = sp_ref[pl.ds(s, CHUNK), :]
        a = dtc * a_row
        tri = (rows >= cols) if forward else (cols >= rows)
        tri_b = jnp.where(tri, 1.0, 0.0).astype(BF16)
        a1 = a.astype(BF16)
        r1 = a - a1.astype(F32)
        a2 = r1.astype(BF16)
        a3 = (r1 - a2.astype(F32)).astype(BF16)
        cs = (jnp.dot(tri_b, a1, preferred_element_type=F32) + jnp.dot(tri_b, a2, preferred_element_type=F32)
              + jnp.dot(tri_b, a3, preferred_element_type=F32))
        cs_t = cs.T
        tot = cs[CHUNK - 1:CHUNK, :] if forward else cs[0:1, :]
        expand = expand_ref[d]
        dt_full = jnp.dot(cat2(dtc), expand, preferred_element_type=F32)
        dout_full = jnp.dot(cat2(jnp.exp(cs)), expand, preferred_element_type=F32)
        dst_full = jnp.dot(cat2(jnp.exp(tot - cs)), expand, preferred_element_type=F32)
        etot_full = dout_full[CHUNK - 1:CHUNK, :] if forward else dout_full[0:1, :]
        colb = jnp.dot(cat2(cs), colsel_ref[d], preferred_element_type=F32)
        xd = xcv[:, :SSD_INNER].astype(F32) * dt_full
        xdb = xd.astype(BF16)
        xdd = (xd * dst_full).astype(BF16)
        b_t = xcv[:, SSD_INNER:SSD_INNER + SSD_GROUPS * D_STATE].astype(F32).T.astype(BF16)
        zero = jnp.zeros((CHUNK, CHUNK), BF16)
        ys = []
        for g in range(SSD_GROUPS):
            bg_t = b_t[g * D_STATE:(g + 1) * D_STATE, :]
            cg = xcv[:, SSD_INNER + (SSD_GROUPS + g) * D_STATE:SSD_INNER + (SSD_GROUPS + g + 1) * D_STATE]
            gmat = jnp.dot(cg, bg_t, preferred_element_type=F32)
            for pp in range(2):
                j = 2 * g + pp
                ms = []
                for hh in range(2):
                    h = 2 * j + hh
                    diff = colb[:, h * CHUNK:(h + 1) * CHUNK] - cs_t[off + h:off + h + 1, :]
                    lm = jnp.exp(jnp.where(tri, diff, -jnp.inf))
                    ms.append((gmat * lm).astype(BF16))
                mcat = jnp.concatenate(ms, axis=1)
                xp = xdb[:, j * CHUNK:(j + 1) * CHUNK]
                xblk = jnp.concatenate([jnp.where(lane_lo, xp, zero), jnp.where(lane_lo, zero, xp)], axis=0)
                y_diag = jnp.dot(mcat, xblk, preferred_element_type=F32)
                st = st_ref[j]
                y_off = jnp.dot(cg, st.astype(BF16), preferred_element_type=F32) * dout_full[:, j * CHUNK:(j + 1) * CHUNK]
                contrib = jnp.dot(bg_t, xdd[:, j * CHUNK:(j + 1) * CHUNK], preferred_element_type=F32)
                st_ref[j] = st * etot_full[:, j * CHUNK:(j + 1) * CHUNK] + contrib
                ys.append(y_diag + y_off)
        return s, xcv, jnp.concatenate(ys, axis=1)

    def fwd_chunk(c, carry):
        s, _, y = scan_chunk(c, True)
        yacc_ref[pl.ds(s, CHUNK), :] = y
        return carry

    def bwd_chunk(i, carry):
        c = n_chunks - 1 - i
        s, xcv, y = scan_chunk(c, False)
        xs = xcv[:, :SSD_INNER].astype(F32)
        y = yacc_ref[pl.ds(s, CHUNK), :] + y + dskip_ref[...] * xs
        y = y * _silu(z_ref[pl.ds(s, CHUNK), :].astype(F32))
        y_ref[pl.ds(s, CHUNK), :] = (_rms(y) * gain_ref[...]).astype(y_ref.dtype)
        return carry

    def store_state(out_ref):
        for j in range(SSD_HEADS // 2):
            sq = jnp.concatenate([st_ref[j], jnp.zeros((2 * SSD_HEADDIM - D_STATE, 2 * SSD_HEADDIM), F32)], axis=0)
            out_ref[j] = sq.T[:, :D_STATE]

    if has_h0:
        st_ref[...] = h0f_ref[...]
    else:
        st_ref[...] = jnp.zeros_like(st_ref)
    lax.fori_loop(0, n_chunks, fwd_chunk, 0)
    store_state(hf_ref)
    if has_h0:
        st_ref[...] = h0b_ref[...]
    else:
        st_ref[...] = jnp.zeros_like(st_ref)
    lax.fori_loop(0, n_chunks, bwd_chunk, 0)
    store_state(hb_ref)


def ssd_mixer(main, small, h0f, h0b, lw, n_seq, seq_len, stack=None):
    t = n_seq * seq_len
    has_h0 = h0f is not None
    pairs = SSD_HEADS // 2
    st_shape = (pairs, D_STATE, 2 * SSD_HEADDIM)
    out_st_shape = (pairs, 2 * SSD_HEADDIM, D_STATE)
    st_spec = pl.BlockSpec((None,) + st_shape, lambda b: (b, 0, 0, 0))
    if has_h0:
        out_st_spec = pl.BlockSpec((None,) + out_st_shape, lambda b: (b, 0, 0, 0))
        out_st = jax.ShapeDtypeStruct((n_seq,) + out_st_shape, F32)
    else:
        depth = stack[0]
        out_st_spec = pl.BlockSpec((None, None) + out_st_shape, lambda b: (b, lw["layer"], 0, 0, 0))
        out_st = jax.ShapeDtypeStruct((n_seq, depth) + out_st_shape, F32)
    full = lambda shape: pl.BlockSpec(shape, lambda b: (0,) * len(shape))
    in_specs = [
        pl.BlockSpec((seq_len, M_Q), lambda b: (b, 0)),
        pl.BlockSpec((seq_len, LANES), lambda b: (b, 0)),
    ]
    args = [main, small]
    if has_h0:
        in_specs += [st_spec, st_spec]
        args += [h0f, h0b]
    k_head = (jnp.arange(2 * LANES, dtype=I32) % LANES)[None, :, None] - SSD_HEADS * jnp.arange(2, dtype=I32)[:, None, None]
    expand = (k_head == (jnp.arange(SSD_INNER, dtype=I32) // SSD_HEADDIM)[None, None, :]).astype(BF16)
    colsel = (k_head == (jnp.arange(SSD_HEADS * CHUNK, dtype=I32) // CHUNK)[None, None, :]).astype(BF16)
    in_specs += [full((8, XBC_DIM)), full((1, XBC_DIM)), full((1, LANES)), full((1, LANES)),
                 full((1, SSD_INNER)), full((1, SSD_INNER)), full((2, 2 * LANES, SSD_INNER)),
                 full((2, 2 * LANES, SSD_HEADS * CHUNK))]
    args += [lw["conv_w"], lw["conv_b"], lw["dt_bias"], lw["a_log"], lw["d_skip"], lw["ssd_norm"], expand, colsel]
    aliases = {}
    if not has_h0:
        aliases = {len(args): 1, len(args) + 1: 2}
        in_specs += [pl.BlockSpec(memory_space=pl.ANY), pl.BlockSpec(memory_space=pl.ANY)]
        args += [stack[1], stack[2]]
    return pl.pallas_call(
        functools.partial(_ssd_kernel, seq_len=seq_len, has_h0=has_h0),
        grid=(n_seq,),
        in_specs=in_specs,
        out_specs=[pl.BlockSpec((seq_len, SSD_INNER), lambda b: (b, 0)), out_st_spec, out_st_spec],
        out_shape=[jax.ShapeDtypeStruct((t, SSD_INNER), BF16), out_st, out_st],
        input_output_aliases=aliases,
        scratch_shapes=[
            pltpu.VMEM((seq_len, XBC_DIM), BF16),
            pltpu.VMEM((seq_len, LANES), F32),
            pltpu.VMEM((seq_len, SSD_INNER), F32),
            pltpu.VMEM(st_shape, F32),
        ],
        compiler_params=_cparams("parallel"),
        name="ssd_mixer",
    )(*args)


def _swap_pairs(x):
    n = x.shape[-1]
    lane = lax.broadcasted_iota(I32, x.shape, x.ndim - 1)
    return jnp.where(lane % 2 == 0, pltpu.roll(x, n - 1, x.ndim - 1), pltpu.roll(x, 1, x.ndim - 1))


def _kv_prep_kernel(*refs, seq_len, rope):
    if rope:
        (k_ref, v_ref, gain_ref, cos_ref, sin_ref, ck_ref, cv_ref, kt_ref, v4_ref) = refs
    else:
        (k_ref, v_ref, gain_ref, kn_ref, vout_ref, kt_ref, v4_ref) = refs[:3] + refs[-4:]
    kf = k_ref[...].astype(F32)
    parts = []
    for g in range(KV_HEADS):
        kh = kf[:, g * HEAD_DIM:(g + 1) * HEAD_DIM]
        parts.append(kh * lax.rsqrt(jnp.mean(kh * kh, axis=-1, keepdims=True) + EPS))
    kn = jnp.concatenate(parts, axis=1) * gain_ref[...]
    if rope:
        kn = kn * cos_ref[...] + _swap_pairs(kn) * sin_ref[...]
    else:
        kn_ref[...] = kn
        vout_ref[...] = v_ref[...].astype(F32)
    kt_ref[:, 0:seq_len] = kn.T.astype(BF16)
    def with_ones(vh):
        return jnp.concatenate([vh, jnp.ones_like(vh)], axis=1).astype(BF16)

    vf = v_ref[...]
    for g in range(KV_HEADS):
        v4_ref[g, 0:seq_len, :] = with_ones(vf[:, g * HEAD_DIM:(g + 1) * HEAD_DIM])
    if rope:
        past = ck_ref.shape[0]
        kt_ref[:, seq_len:seq_len + past] = ck_ref[...].T.astype(BF16)
        cvf = cv_ref[...]
        for g in range(KV_HEADS):
            v4_ref[g, seq_len:seq_len + past, :] = with_ones(cvf[:, g * HEAD_DIM:(g + 1) * HEAD_DIM])


def kv_prep(main, k_gain, n_seq, seq_len, rope_kv=None, cache=None, stack=None):
    rope = rope_kv is not None
    aliases = {}
    in_specs = [
        pl.BlockSpec((seq_len, KV_INNER), lambda b: (b, M_K // KV_INNER)),
        pl.BlockSpec((seq_len, KV_INNER), lambda b: (b, M_V // KV_INNER)),
        pl.BlockSpec((1, KV_INNER), lambda b: (0, 0)),
    ]
    args = [main, main, k_gain]
    n_keys = seq_len
    out_specs, out_shape = [], []
    if rope:
        cache_k, cache_v, layer = cache
        past = cache_k.shape[2]
        n_keys += past
        in_specs += [
            pl.BlockSpec((seq_len, KV_INNER), lambda b: (0, 0)),
            pl.BlockSpec((seq_len, KV_INNER), lambda b: (0, 0)),
            pl.BlockSpec((None, None, past, KV_INNER), lambda b: (b, layer, 0, 0)),
            pl.BlockSpec((None, None, past, KV_INNER), lambda b: (b, layer, 0, 0)),
        ]
        args += [rope_kv[0], rope_kv[1], cache_k, cache_v]
    else:
        layer, depth, k_prev, v_prev = stack
        aliases = {len(args): 0, len(args) + 1: 1}
        in_specs += [pl.BlockSpec(memory_space=pl.ANY), pl.BlockSpec(memory_space=pl.ANY)]
        args += [k_prev, v_prev]
        for _ in range(2):
            out_specs.append(pl.BlockSpec((None, None, seq_len, KV_INNER), lambda b: (b, layer, 0, 0)))
            out_shape.append(jax.ShapeDtypeStruct((n_seq, depth, seq_len, KV_INNER), F32))
    out_specs += [
        pl.BlockSpec((None, KV_INNER, n_keys), lambda b: (b, 0, 0)),
        pl.BlockSpec((None, KV_HEADS, n_keys, 2 * HEAD_DIM), lambda b: (b, 0, 0, 0)),
    ]
    out_shape += [
        jax.ShapeDtypeStruct((n_seq, KV_INNER, n_keys), BF16),
        jax.ShapeDtypeStruct((n_seq, KV_HEADS, n_keys, 2 * HEAD_DIM), BF16),
    ]
    return pl.pallas_call(
        functools.partial(_kv_prep_kernel, seq_len=seq_len, rope=rope),
        grid=(n_seq,),
        in_specs=in_specs,
        out_specs=out_specs,
        out_shape=out_shape,
        input_output_aliases=aliases,
        compiler_params=_cparams("parallel"),
        name="kv_prep",
    )(*args)


def _attn_kernel(*refs, rope, tq):
    if rope:
        qa_ref, qb_ref, gain_ref, cos_ref, sin_ref, kt_ref, v4_ref, o_ref = refs
    else:
        qa_ref, qb_ref, gain_ref, kt_ref, v4_ref, o_ref = refs
    rep = N_HEADS // KV_HEADS
    scale = HEAD_DIM ** -0.5
    slabs = ATTN_INNER // LANES // 2
    heads = []
    for j in range(N_HEADS // 2):
        q_ref = qa_ref if j < slabs else qb_ref
        qs = q_ref[:, (j % slabs) * LANES:(j % slabs + 1) * LANES].astype(F32)
        qg = qs * gain_ref[...]
        if rope:
            qg = qg * cos_ref[...] + _swap_pairs(qg) * sin_ref[...]
        for hh in range(2):
            raw = qs[:, hh * HEAD_DIM:(hh + 1) * HEAD_DIM]
            inv = lax.rsqrt(jnp.mean(raw * raw, axis=-1, keepdims=True) + EPS) * scale
            heads.append((qg[:, hh * HEAD_DIM:(hh + 1) * HEAD_DIM] * inv).astype(BF16))
    for g in range(KV_HEADS):
        qstack = jnp.concatenate(heads[g * rep:(g + 1) * rep], axis=0)
        kt = kt_ref[g * HEAD_DIM:(g + 1) * HEAD_DIM, :]
        s = jnp.dot(qstack, kt, preferred_element_type=F32)
        m = jnp.max(s, axis=-1, keepdims=True)
        p = jnp.exp(s - m).astype(BF16)
        ov = jnp.dot(p, v4_ref[g], preferred_element_type=F32)
        o = ov[:, :HEAD_DIM] / ov[:, HEAD_DIM:HEAD_DIM + 1]
        for r in range(rep):
            h = g * rep + r
            o_ref[:, h * HEAD_DIM:(h + 1) * HEAD_DIM] = o[r * tq:(r + 1) * tq, :].astype(o_ref.dtype)


def attention(main, q_gain, kt, v4, n_seq, seq_len, tq, rope_q=None):
    t = n_seq * seq_len
    nq = seq_len // tq
    n_keys = kt.shape[2]
    rope = rope_q is not None
    half = ATTN_INNER // 2
    in_specs = [
        pl.BlockSpec((tq, half), lambda b, i: (b * nq + i, M_Q // half)),
        pl.BlockSpec((tq, half), lambda b, i: (b * nq + i, M_Q // half + 1)),
        pl.BlockSpec((1, LANES), lambda b, i: (0, 0)),
    ]
    args = [main, main, q_gain]
    if rope:
        in_specs += [pl.BlockSpec((tq, LANES), lambda b, i: (i, 0)), pl.BlockSpec((tq, LANES), lambda b, i: (i, 0))]
        args += [rope_q[0], rope_q[1]]
    in_specs += [
        pl.BlockSpec((None, KV_INNER, n_keys), lambda b, i: (b, 0, 0)),
        pl.BlockSpec((None, KV_HEADS, n_keys, 2 * HEAD_DIM), lambda b, i: (b, 0, 0, 0)),
    ]
    args += [kt, v4]
    return pl.pallas_call(
        functools.partial(_attn_kernel, rope=rope, tq=tq),
        grid=(n_seq, nq),
        in_specs=in_specs,
        out_specs=pl.BlockSpec((tq, ATTN_INNER), lambda b, i: (b * nq + i, 0)),
        out_shape=jax.ShapeDtypeStruct((t, ATTN_INNER), BF16),
        compiler_params=_cparams("parallel", "arbitrary"),
        name="attention",
    )(*args)


def _pool_kernel(p_ref, w_ref, scale_ref, o_ref, *, seq_len):
    n_chunks = seq_len // CHUNK
    win = CHUNK + 2 * HALO
    t_loc = lax.broadcasted_iota(I32, (CHUNK, win), 0)
    r_loc = lax.broadcasted_iota(I32, (CHUNK, win), 1) - HALO
    t_col = lax.broadcasted_iota(I32, (CHUNK, 1), 0)

    def chunk(c, carry):
        s = pl.multiple_of(c * CHUNK, CHUNK)
        w = _chunk_window(p_ref, c, n_chunks, seq_len).astype(BF16)
        outs = []
        for g, width in enumerate(POOL_WINDOWS):
            half = width // 2
            d = r_loc - t_loc
            band = jnp.where((d >= -half) & (d < width - half), 1.0, 0.0).astype(BF16)
            wg = w[:, g * POOL_GROUP_DIM:(g + 1) * POOL_GROUP_DIM]
            sums = jnp.dot(band, wg, preferred_element_type=F32)
            tg = t_col + s
            cnt = (jnp.minimum(tg + (width - half), seq_len) - jnp.maximum(tg - half, 0)).astype(F32)
            cur = wg[HALO:HALO + CHUNK, :].astype(F32)
            diff = sums / cnt - cur
            outs.append(jnp.dot(diff.astype(BF16), w_ref[g], preferred_element_type=F32))
        o_ref[pl.ds(s, CHUNK), :] = (jnp.concatenate(outs, axis=1) * scale_ref[...]).astype(o_ref.dtype)
        return carry

    lax.fori_loop(0, n_chunks, chunk, 0)


def pool_mixer(main, pool_w, layer, pool_scale, n_seq, seq_len):
    t = n_seq * seq_len
    return pl.pallas_call(
        functools.partial(_pool_kernel, seq_len=seq_len),
        grid=(n_seq,),
        in_specs=[
            pl.BlockSpec((seq_len, POOL_INNER), lambda b: (b, M_PIN // POOL_INNER)),
            pl.BlockSpec((None, POOL_GROUPS, POOL_GROUP_DIM, POOL_GROUP_DIM), lambda b: (layer, 0, 0, 0)),
            pl.BlockSpec((1, POOL_INNER), lambda b: (0, 0)),
        ],
        out_specs=pl.BlockSpec((seq_len, POOL_INNER), lambda b: (b, 0)),
        out_shape=jax.ShapeDtypeStruct((t, POOL_INNER), BF16),
        compiler_params=_cparams("parallel"),
        name="pool_mixer",
    )(main, pool_w, pool_scale)


def _merge_kernel(ys_ref, ya_ref, yp_ref, g0a_ref, g0b_ref, g1a_ref, g1b_ref, g2a_ref, g2b_ref, ws_ref, wa_ref,
                  wp_ref, o_ref):
    def gate(a_ref, b_ref):
        return jax.nn.sigmoid(jnp.concatenate([a_ref[...], b_ref[...]], axis=1).astype(F32))

    acc = gate(g0a_ref, g0b_ref) * jnp.dot(ys_ref[...], ws_ref[...], preferred_element_type=F32)
    acc = acc + gate(g1a_ref, g1b_ref) * jnp.dot(ya_ref[...], wa_ref[...], preferred_element_type=F32)
    acc = acc + gate(g2a_ref, g2b_ref) * jnp.dot(yp_ref[...], wp_ref[...], preferred_element_type=F32)
    o_ref[...] = acc.astype(o_ref.dtype)


def merge_branches(y_ssd, o_attn, o_pool, main, lw, tm=512):
    t = y_ssd.shape[0]
    half = D_MODEL // 2
    act = lambda: pl.BlockSpec((tm, SSD_INNER), lambda i: (i, 0))
    gate = lambda k: pl.BlockSpec((tm, half), lambda i: (i, M_GATE // half + k))
    wspec = lambda: pl.BlockSpec((None, SSD_INNER, D_MODEL), lambda i: (lw["layer"], 0, 0))
    return pl.pallas_call(
        _merge_kernel,
        grid=(t // tm,),
        in_specs=[act(), act(), act()] + [gate(k) for k in range(6)] + [wspec(), wspec(), wspec()],
        out_specs=pl.BlockSpec((tm, D_MODEL), lambda i: (i, 0)),
        out_shape=jax.ShapeDtypeStruct((t, D_MODEL), BF16),
        compiler_params=_cparams("parallel"),
        name="merge_branches",
    )(y_ssd, o_attn, o_pool, *([main] * 6), lw["w_ssd_out"], lw["w_attn_out"], lw["w_pool_out"])


def _route(logits_t, bias_col):
    scores = jax.nn.sigmoid(logits_t)
    biased = scores + bias_col
    sc = [scores[e:e + 1, :] for e in range(N_EXPERTS)]
    bi = [biased[e:e + 1, :] for e in range(N_EXPERTS)]
    group_scores = []
    for g in range(N_EXPERT_GROUPS):
        a, b, c, d = bi[4 * g:4 * g + 4]
        hi1, lo1 = jnp.maximum(a, b), jnp.minimum(a, b)
        hi2, lo2 = jnp.maximum(c, d), jnp.minimum(c, d)
        top1 = jnp.maximum(hi1, hi2)
        top2 = jnp.maximum(jnp.minimum(hi1, hi2), jnp.maximum(lo1, lo2))
        group_scores.append(top1 + top2)
    best = jnp.zeros_like(group_scores[0], dtype=I32)
    best_v = group_scores[0]
    for g in range(1, N_EXPERT_GROUPS):
        upd = group_scores[g] > best_v
        best = jnp.where(upd, g, best)
        best_v = jnp.where(upd, group_scores[g], best_v)
    vb, vs = [], []
    for j in range(EXPERTS_PER_GROUP):
        b_j, s_j = bi[j], sc[j]
        for g in range(1, N_EXPERT_GROUPS):
            sel = best == g
            b_j = jnp.where(sel, bi[4 * g + j], b_j)
            s_j = jnp.where(sel, sc[4 * g + j], s_j)
        vb.append(b_j)
        vs.append(s_j)

    def first_argmax(vals, excluded=None):
        idx = None
        for j in range(EXPERTS_PER_GROUP):
            v = vals[j] if excluded is None else jnp.where(excluded == j, -jnp.inf, vals[j])
            if idx is None:
                idx, cur = jnp.zeros_like(best), v
            else:
                upd = v > cur
                idx = jnp.where(upd, j, idx)
                cur = jnp.where(upd, v, cur)
        return idx

    i0 = first_argmax(vb)
    i1 = first_argmax(vb, excluded=i0)

    def pick(vals, idx):
        out = vals[0]
        for j in range(1, EXPERTS_PER_GROUP):
            out = jnp.where(idx == j, vals[j], out)
        return out

    s0, s1 = pick(vs, i0), pick(vs, i1)
    tot = s0 + s1
    ids = jnp.concatenate([best * EXPERTS_PER_GROUP + i0, best * EXPERTS_PER_GROUP + i1], axis=0)
    wts = jnp.concatenate([s0 / tot, s1 / tot], axis=0)
    return ids, wts


def _outproj_kernel(x_ref, m_ref, wo_ref, g1_ref, gain_ref, sh_ref, sc_ref, wr_ref, wr_hi_ref, br_ref,
                    xo_ref, h_ref, ids_ref, rank_ref, wcol_ref, cnt_ref, run_ref):
    tm = x_ref.shape[0]

    @pl.when(pl.program_id(0) == 0)
    def _():
        run_ref[...] = jnp.zeros_like(run_ref)

    mix = jnp.dot(m_ref[...], wo_ref[...], preferred_element_type=F32)
    xn = x_ref[...] + g1_ref[...] * mix
    xo_ref[...] = xn
    h = _rms(xn) * gain_ref[...] * (1.0 + sc_ref[...]) + sh_ref[...]
    h_hi, h_lo = _split_bf16(h)
    for j in range(ROW_TILES):
        h_ref[pl.ds(j, tm, stride=ROW_TILES), :] = h[:, j * LANES:(j + 1) * LANES]
    lg = jnp.dot(h_hi, wr_ref[...], preferred_element_type=F32) + jnp.dot(h_lo, wr_hi_ref[...],
                                                                         preferred_element_type=F32)
    lg_t = lg.T
    logits_t = lg_t[0:N_EXPERTS, :] + lg_t[N_EXPERTS:2 * N_EXPERTS, :]
    ids, wts = _route(logits_t, br_ref[...])

    def fold(rows):
        return jnp.concatenate([rows[k:k + 1, j * LANES:(j + 1) * LANES] for k in range(2)
                                for j in range(tm // LANES)], axis=0)

    ids_ref[...] = fold(ids)
    e_iota = lax.broadcasted_iota(I32, (N_EXPERTS, tm), 0)
    earlier = (lax.broadcasted_iota(I32, (tm, tm), 0) < lax.broadcasted_iota(I32, (tm, tm), 1))
    earlier = jnp.where(earlier, 1.0, 0.0).astype(BF16)
    oh0 = jnp.where(e_iota == ids[0:1, :], 1.0, 0.0)
    oh1 = jnp.where(e_iota == ids[1:2, :], 1.0, 0.0)
    before0 = jnp.dot(oh0.astype(BF16), earlier, preferred_element_type=F32)
    before1 = jnp.dot(oh1.astype(BF16), earlier, preferred_element_type=F32)
    c0 = jnp.sum(oh0, axis=1, keepdims=True)
    c1 = jnp.sum(oh1, axis=1, keepdims=True)
    running = run_ref[...]
    rank0 = jnp.sum(oh0 * (before0 + running), axis=0, keepdims=True)
    rank1 = jnp.sum(oh1 * (before1 + running + c0), axis=0, keepdims=True)
    rank_ref[...] = fold(jnp.concatenate([rank0, rank1], axis=0).astype(I32))
    running = running + c0 + c1
    run_ref[...] = running
    cnt_ref[...] = jnp.broadcast_to(running, cnt_ref.shape)
    r_iota = lax.broadcasted_iota(I32, (LANES, tm), 0)
    w_rows = jnp.where(r_iota == 0, wts[0:1, :], jnp.where(r_iota == 1, wts[1:2, :], 0.0))
    wcol_ref[...] = w_rows.T


def outproj_route(x, merged, mods, lw, gw, path):
    t = x.shape[0]
    tm = ROUTE_TILE
    rf = _row_fn(path, tm)
    row = lambda: pl.BlockSpec((1, D_MODEL), lambda i: (0, 0))
    return pl.pallas_call(
        _outproj_kernel,
        grid=(t // tm,),
        in_specs=[
            pl.BlockSpec((tm, D_MODEL), lambda i: (i, 0)),
            pl.BlockSpec((tm, D_MODEL), lambda i: (i, 0)),
            pl.BlockSpec((None, D_MODEL, D_MODEL), lambda i: (lw["layer"], 0, 0)),
            _mod_spec(2, rf),
            row(),
            _mod_spec(3, rf),
            _mod_spec(4, rf),
            pl.BlockSpec((D_MODEL, LANES), lambda i: (0, 0)),
            pl.BlockSpec((D_MODEL, LANES), lambda i: (0, 0)),
            pl.BlockSpec((N_EXPERTS, 1), lambda i: (0, 0)),
        ],
        out_specs=[
            pl.BlockSpec((tm, D_MODEL), lambda i: (i, 0)),
            pl.BlockSpec((tm * ROW_TILES, LANES), lambda i: (i, 0)),
            pl.BlockSpec((None, 2 * tm // LANES, LANES), lambda i: (i, 0, 0)),
            pl.BlockSpec((None, 2 * tm // LANES, LANES), lambda i: (i, 0, 0)),
            pl.BlockSpec((tm, LANES), lambda i: (i, 0)),
            pl.BlockSpec((N_EXPERTS, LANES), lambda i: (0, 0)),
        ],
        out_shape=[
            jax.ShapeDtypeStruct((t, D_MODEL), F32),
            jax.ShapeDtypeStruct((t * ROW_TILES, LANES), F32),
            jax.ShapeDtypeStruct((t // tm, 2 * tm // LANES, LANES), I32),
            jax.ShapeDtypeStruct((t // tm, 2 * tm // LANES, LANES), I32),
            jax.ShapeDtypeStruct((t, LANES), F32),
            jax.ShapeDtypeStruct((N_EXPERTS, LANES), F32),
        ],
        scratch_shapes=[pltpu.VMEM((N_EXPERTS, 1), F32)],
        compiler_params=_cparams("arbitrary"),
        name="outproj_route",
    )(x, merged, lw["w_out"], mods, lw["norm_ffn"], mods, mods, gw["w_router_cat"], gw["w_router_hi"],
      gw["b_router"])


def _moe_kernel(tile_expert_ref, src_ref, n_used_ref, h_hbm, wg_ref, wu_ref, wd_ref, o_ref, *scratch):
    i = pl.program_id(0)
    n_used = n_used_ref[0]
    bufs = scratch[:MOE_BUFS]
    sem = scratch[MOE_BUFS]

    def start_tile(tile, slot):
        for r in range(MOE_TILE):
            row = pl.multiple_of(src_ref[tile * MOE_TILE + r] * ROW_TILES, ROW_TILES)
            pltpu.make_async_copy(h_hbm.at[pl.ds(row, ROW_TILES)], bufs[slot].at[pl.ds(r * ROW_TILES, ROW_TILES)],
                                  sem.at[slot]).start(priority=r % 2)

    def wait_tile(slot):
        pltpu.make_async_copy(h_hbm.at[pl.ds(0, MOE_TILE * ROW_TILES)], bufs[slot], sem.at[slot]).wait()

    def compute(slot):
        x = jnp.concatenate([bufs[slot][pl.ds(j, MOE_TILE, stride=ROW_TILES), :].astype(BF16)
                             for j in range(ROW_TILES)], axis=1)
        a = jnp.dot(x, wg_ref[...], preferred_element_type=F32)
        u = jnp.dot(x, wu_ref[...], preferred_element_type=F32)
        act = _silu(a) * u
        o_ref[...] = jnp.dot(act.astype(BF16), wd_ref[...], preferred_element_type=F32).astype(o_ref.dtype)

    @pl.when(i == 0)
    def _():
        for t0 in range(MOE_AHEAD):
            @pl.when(t0 < n_used)
            def _(t0=t0):
                start_tile(t0, t0)

    for slot in range(MOE_BUFS):
        @pl.when((i % MOE_BUFS == slot) & (i + MOE_AHEAD < n_used))
        def _(slot=slot):
            wait_tile(slot)
            start_tile(i + MOE_AHEAD, (slot + MOE_AHEAD) % MOE_BUFS)
            compute(slot)

        @pl.when((i % MOE_BUFS == slot) & (i + MOE_AHEAD >= n_used) & (i < n_used))
        def _(slot=slot):
            wait_tile(slot)
            compute(slot)

    @pl.when(i >= n_used)
    def _():
        o_ref[...] = jnp.zeros_like(o_ref)


def moe_experts(h, src, tile_expert, n_used, lw):
    rows = src.shape[0]
    n_tiles = rows // MOE_TILE
    layer = lw["layer"]
    grid_spec = pltpu.PrefetchScalarGridSpec(
        num_scalar_prefetch=3,
        grid=(n_tiles,),
        in_specs=[
            pl.BlockSpec(memory_space=pl.ANY),
            pl.BlockSpec((None, None, D_MODEL, D_FF), lambda i, te, sr, nu: (layer, te[i], 0, 0)),
            pl.BlockSpec((None, None, D_MODEL, D_FF), lambda i, te, sr, nu: (layer, te[i], 0, 0)),
            pl.BlockSpec((None, None, D_FF, D_MODEL), lambda i, te, sr, nu: (layer, te[i], 0, 0)),
        ],
        out_specs=pl.BlockSpec((MOE_TILE, D_MODEL), lambda i, te, sr, nu: (i, 0)),
        scratch_shapes=[pltpu.VMEM((MOE_TILE * ROW_TILES, LANES), F32) for _ in range(MOE_BUFS)]
        + [pltpu.SemaphoreType.DMA((MOE_BUFS,))],
    )
    return pl.pallas_call(
        _moe_kernel,
        grid_spec=grid_spec,
        out_shape=jax.ShapeDtypeStruct((rows, D_MODEL), BF16),
        compiler_params=_cparams("arbitrary"),
        name="moe_experts",
    )(tile_expert, src, n_used, h, lw["w_gate_ff"], lw["w_up_ff"], lw["w_down_ff"])


def _moe_plan(ids, rank, counts):
    two_t = ids.size
    ids = ids.reshape(two_t)
    rank = rank.reshape(two_t)
    counts = counts[:, 0].astype(I32)
    padded = ((counts + MOE_TILE - 1) // MOE_TILE) * MOE_TILE
    ends = jnp.cumsum(padded)
    starts = ends - padded
    onehot = ids[:, None] == jnp.arange(N_EXPERTS, dtype=I32)[None, :]
    pos = jnp.sum(jnp.where(onehot, starts[None, :], 0), axis=1) + rank
    n_rows = two_t + N_EXPERTS * MOE_TILE
    n_tiles = n_rows // MOE_TILE
    pair = jnp.arange(two_t, dtype=I32)
    token = (pair // (2 * ROUTE_TILE)) * ROUTE_TILE + pair % ROUTE_TILE
    filler = jnp.arange(n_rows, dtype=I32) % (two_t // 2)
    src = filler.at[pos].set(token, mode="promise_in_bounds", unique_indices=True)
    tile_start = jnp.arange(n_tiles, dtype=I32) * MOE_TILE
    tile_expert = jnp.minimum(jnp.sum((tile_start[:, None] >= ends[None, :]).astype(I32), axis=1), N_EXPERTS - 1)
    n_used = (ends[-1] // MOE_TILE).reshape(1).astype(I32)
    return pos, src, tile_expert.astype(I32), n_used


def moe_ffn(h2, ids, rank, counts, lw):
    pos, src, tile_expert, n_used = _moe_plan(ids, rank, counts)
    ys = moe_experts(h2, src, tile_expert, n_used, lw)
    return ys.at[pos].get(mode="promise_in_bounds", unique_indices=True)


def _rope_tables(n_tokens):
    rows = n_tokens // GRID_W
    row = jnp.repeat(jnp.arange(rows, dtype=F32), GRID_W)
    col = jnp.tile(jnp.arange(GRID_W, dtype=F32), rows)
    n_freq = HEAD_DIM // 4
    inv = ROPE_THETA ** (-jnp.arange(n_freq, dtype=F32) / n_freq)
    ang = jnp.concatenate([row[:, None] * inv, col[:, None] * inv], axis=-1)
    cos = jnp.repeat(jnp.cos(ang), 2, axis=-1)
    sin = jnp.repeat(jnp.sin(ang), 2, axis=-1)
    sign = jnp.tile(jnp.array([-1.0, 1.0], F32), HEAD_DIM // 2)
    sin = sin * sign
    return cos, sin


def _prep_stacked_weights(p):
    w_in = p["w_in"]
    depth = w_in.shape[0]
    w_main = jnp.concatenate([w_in[:, :, :_O_DTF], w_in[:, :, _O_Q:]], axis=2).astype(BF16)
    w_small = jnp.concatenate([w_in[:, :, _O_DTF:_O_Q],
                               jnp.zeros((depth, D_MODEL, LANES - 2 * SSD_HEADS), F32)], axis=2).astype(BF16)
    out = {"w_main": w_main, "w_small": w_small}
    for name in ("pool_w", "w_ssd_out", "w_attn_out", "w_pool_out", "w_out", "w_gate_ff", "w_up_ff", "w_down_ff"):
        out[name] = p[name].astype(BF16)
    return out


def _prep_layer_weights(l, p, stacked):
    pad_lanes = lambda v: jnp.concatenate([v.reshape(-1), jnp.zeros((LANES - v.size,), F32)]).reshape(1, LANES)
    return {
        **stacked,
        "layer": l,
        "conv_w": jnp.concatenate([p["conv_w"][l], jnp.zeros((8 - CONV_W, XBC_DIM), F32)], axis=0),
        "conv_b": p["conv_b"][l].reshape(1, XBC_DIM),
        "dt_bias": pad_lanes(p["dt_bias"][l]),
        "a_log": pad_lanes(p["a_log"][l]),
        "d_skip": jnp.repeat(p["d_skip"][l], SSD_HEADDIM).reshape(1, SSD_INNER),
        "ssd_norm": p["ssd_norm"][l].reshape(1, SSD_INNER),
        "q_gain": jnp.tile(p["q_norm"][l], LANES // HEAD_DIM).reshape(1, LANES),
        "k_gain": jnp.tile(p["k_norm"][l], KV_HEADS).reshape(1, KV_INNER),
        "pool_scale": p["pool_scale"][l].reshape(1, POOL_INNER),
        "norm_mix": p["norm_mix"][l].reshape(1, D_MODEL),
        "norm_ffn": p["norm_ffn"][l].reshape(1, D_MODEL),
    }


def _mixer_stage(h, x, mods, lw, gw, path, n_seq, seq_len, tq, h0, rope, cache, outs):
    t = n_seq * seq_len
    main = matmul(h, lw["w_main"], lw["layer"], BF16, min(t, 2048), MAIN_TN, "in_proj_main")
    small = matmul(h, lw["w_small"], lw["layer"], F32, min(t, 2048), LANES, "in_proj_dt")
    if rope is None:
        depth = outs["k"].shape[1]
        y_ssd, hf, hb = ssd_mixer(main, small, None, None, lw, n_seq, seq_len, stack=(depth, outs["hf"], outs["hb"]))
        k_st, v_st, kt, v4 = kv_prep(main, lw["k_gain"], n_seq, seq_len,
                                     stack=(lw["layer"], depth, outs["k"], outs["v"]))
        outs = {"k": k_st, "v": v_st, "hf": hf, "hb": hb}
        o_attn = attention(main, lw["q_gain"], kt, v4, n_seq, seq_len, tq)
    else:
        y_ssd, _, _ = ssd_mixer(main, small, h0[0], h0[1], lw, n_seq, seq_len)
        kt, v4 = kv_prep(main, lw["k_gain"], n_seq, seq_len, rope_kv=rope[0], cache=cache)
        o_attn = attention(main, lw["q_gain"], kt, v4, n_seq, seq_len, tq, rope_q=rope[1])
    o_pool = pool_mixer(main, lw["pool_w"], lw["layer"], lw["pool_scale"], n_seq, seq_len)
    merged = merge_branches(y_ssd, o_attn, o_pool, main, lw)
    x_new, h2, ids, rank, wcol, counts = outproj_route(x, merged, mods, lw, gw, path)
    y2 = moe_ffn(h2, ids, rank, counts, lw)
    return x_new, y2, wcol, outs


def kernel(x_prompt, x_sample, cache_k, cache_v, state_ssd_fwd, state_ssd_bwd, c, c_ctx, w_ada, b_ada, norm_mix,
           norm_ffn, w_in, conv_w, conv_b, a_log, dt_bias, d_skip, ssd_norm, q_norm, k_norm, pool_w, pool_scale,
           w_ssd_out, w_attn_out, w_pool_out, w_out, w_router, b_router, w_gate_ff, w_up_ff, w_down_ff, norm_final):
    p = dict(norm_mix=norm_mix, norm_ffn=norm_ffn, w_in=w_in, conv_w=conv_w, conv_b=conv_b, a_log=a_log,
             dt_bias=dt_bias, d_skip=d_skip, ssd_norm=ssd_norm, q_norm=q_norm, k_norm=k_norm, pool_w=pool_w,
             pool_scale=pool_scale, w_ssd_out=w_ssd_out, w_attn_out=w_attn_out, w_pool_out=w_pool_out, w_out=w_out,
             w_gate_ff=w_gate_ff, w_up_ff=w_up_ff, w_down_ff=w_down_ff)
    bp, lc, _ = x_prompt.shape
    depth = w_in.shape[0]
    bs, ls, _ = x_sample.shape
    past = cache_k.shape[2]
    tc, tl = bp * lc, bs * ls

    mod_rows = -(-(1 + bs) // 8) * 8
    cond = jnp.concatenate([c_ctx[None, :], c, jnp.zeros((mod_rows - 1 - bs, D_MODEL), F32)], axis=0)
    mods_all = ada_mods(cond, w_ada, b_ada).reshape(depth, mod_rows, 6, 1, D_MODEL)

    wr_hi, wr_lo = _split_bf16(w_router)
    zpad = jnp.zeros((D_MODEL, LANES - 2 * N_EXPERTS), BF16)
    gw = {
        "w_router_cat": jnp.concatenate([wr_hi, wr_lo, zpad], axis=1),
        "w_router_hi": jnp.concatenate([wr_hi, jnp.zeros((D_MODEL, LANES - N_EXPERTS), BF16)], axis=1),
        "b_router": b_router.reshape(N_EXPERTS, 1),
    }
    cos, sin = _rope_tables(ls)
    rope = ((jnp.tile(cos, (1, KV_HEADS)), jnp.tile(sin, (1, KV_HEADS))),
            (jnp.tile(cos, (1, LANES // HEAD_DIM)), jnp.tile(sin, (1, LANES // HEAD_DIM))))
    cache_k4 = cache_k.reshape(bs, depth, past, KV_INNER)
    cache_v4 = cache_v.reshape(bs, depth, past, KV_INNER)
    pairs = SSD_HEADS // 2

    def pack_state(st):
        st = st.reshape(bs, depth, pairs, 2, SSD_HEADDIM, D_STATE)
        return st.transpose(0, 1, 2, 5, 3, 4).reshape(bs, depth, pairs, D_STATE, 2 * SSD_HEADDIM)

    st_f = pack_state(state_ssd_fwd)
    st_b = pack_state(state_ssd_bwd)

    path_c = (0, False, lc)
    path_l = (1, True, ls)
    xc = x_prompt.reshape(tc, D_MODEL)
    xl = x_sample.reshape(tl, D_MODEL)
    outs = {"k": jnp.zeros((bp, depth, lc, KV_INNER), F32), "v": jnp.zeros((bp, depth, lc, KV_INNER), F32),
            "hf": jnp.zeros((bp, depth, pairs, 2 * SSD_HEADDIM, D_STATE), F32),
            "hb": jnp.zeros((bp, depth, pairs, 2 * SSD_HEADDIM, D_STATE), F32)}
    hc = hl = None
    y2c = y2l = wc = wl = None
    stacked = _prep_stacked_weights(p)
    for l in range(depth):
        lw = _prep_layer_weights(l, p, stacked)
        mods = mods_all[l]
        if l == 0:
            hc = modulate_first(xc, mods, lw["norm_mix"], path_c)
            hl = modulate_first(xl, mods, lw["norm_mix"], path_l)
        else:
            xc, hc = residual_modulate(xc, y2c, wc, mods_all[l - 1], mods, lw["norm_mix"], path_c)
            xl, hl = residual_modulate(xl, y2l, wl, mods_all[l - 1], mods, lw["norm_mix"], path_l)
        xc, y2c, wc, outs = _mixer_stage(hc, xc, mods, lw, gw, path_c, bp, lc, lc, (None, None), None, None, outs)
        xl, y2l, wl, _ = _mixer_stage(
            hl, xl, mods, lw, gw, path_l, bs, ls, CHUNK, (st_f[:, l], st_b[:, l]), rope, (cache_k4, cache_v4, l), None)
    gain_f = norm_final.reshape(1, D_MODEL)
    y_prompt = residual_final(xc, y2c, wc, mods_all[depth - 1], gain_f, path_c).reshape(bp, lc, D_MODEL)
    y_sample = residual_final(xl, y2l, wl, mods_all[depth - 1], gain_f, path_l).reshape(bs, ls, D_MODEL)
    return (y_prompt, y_sample, outs["k"].reshape(bp, depth, lc, KV_HEADS, HEAD_DIM),
            outs["v"].reshape(bp, depth, lc, KV_HEADS, HEAD_DIM),
            outs["hf"].reshape(bp, depth, SSD_HEADS, SSD_HEADDIM, D_STATE),
            outs["hb"].reshape(bp, depth, SSD_HEADS, SSD_HEADDIM, D_STATE))
```
